```python
import math
import jax, jax.numpy as jnp
from jax import lax
import numpy as np

D_MODEL = 1024
BATCH = 8
SEQ = 2048
DEPTH = 4
DEC_BATCH = 128
DEC_SEQ = 4
PAST_LEN = 16384
PAGE_SIZE = 128

SC_DIM = D_MODEL
SC_CONV_W = 3
SSD_EXPAND = 2
SSD_INNER = SSD_EXPAND * D_MODEL
SSD_HEADDIM = 64
SSD_HEADS = SSD_INNER // SSD_HEADDIM
SSD_GROUPS = 4
SSD_STATE = 128
SSD_CONV_W = 4
SSD_CONV_DIM = SSD_INNER + 2 * SSD_GROUPS * SSD_STATE
SSD_CHUNK = 128
MEM_LEN = 256
MEM_HEADS = 4
MEM_HEAD_DIM = D_MODEL // MEM_HEADS
MEM_DIM = MEM_HEADS * MEM_HEAD_DIM
N_BRANCHES = 3
IN_SIZES = (SC_DIM, SC_DIM, SC_DIM, SSD_INNER, SSD_CONV_DIM, SSD_HEADS, MEM_DIM, N_BRANCHES * D_MODEL)
IN_PROJ_DIM = 3 * SC_DIM + SSD_INNER + SSD_CONV_DIM + SSD_HEADS + MEM_DIM + N_BRANCHES * D_MODEL
N_EXP_GROUPS = 4
EXP_PER_GROUP = 8
N_EXPERTS = N_EXP_GROUPS * EXP_PER_GROUP
EXP_TOPK = 2
EXP_FF = 512
MOE_BLOCK = 128
ALPHA = (2 * DEPTH) ** 0.25
BETA = (8 * DEPTH) ** -0.25
LN_EPS = 1e-5
RMS_EPS = 1e-6

kernel_name = "hybrid_sconv_ssd_memxattn_hiermoe_deepnorm_step"


def _split_points(sizes):
    pts, acc = [], 0
    for s in sizes[:-1]:
        acc += s
        pts.append(acc)
    return pts


def layer_norm(x, g, b):
    xf = x.astype(jnp.float32)
    mu = jnp.mean(xf, axis=-1, keepdims=True)
    var = jnp.mean(jnp.square(xf - mu), axis=-1, keepdims=True)
    y = (xf - mu) * lax.rsqrt(var + LN_EPS) * g.astype(jnp.float32) + b.astype(jnp.float32)
    return y.astype(x.dtype)


def causal_depthwise_conv(u, prev, w):
    width = w.shape[0]
    L = u.shape[1]
    ext = jnp.concatenate([prev.astype(u.dtype), u], axis=1)
    out = ext[:, 0:L] * w[0]
    for k in range(1, width):
        out = out + ext[:, k:k + L] * w[k]
    return out, ext[:, L:]


def gated_rmsnorm(y, z, w):
    yf = y.astype(jnp.float32) * jax.nn.silu(z.astype(jnp.float32))
    yg = yf.reshape(y.shape[:-1] + (SSD_GROUPS, SSD_INNER // SSD_GROUPS))
    yg = yg * lax.rsqrt(jnp.mean(jnp.square(yg), axis=-1, keepdims=True) + RMS_EPS)
    return yg.reshape(y.shape) * w.astype(jnp.float32)


def ssd_chunked_scan(xs, dt, a, bm, cm, h0):
    b, L, H, P = xs.shape
    G, N = bm.shape[2], bm.shape[3]
    Hg = H // G
    f32 = jnp.float32
    Q = min(SSD_CHUNK, L)
    nc = -(-L // Q)
    pad = nc * Q - L
    xdt = (xs.astype(f32) * dt[..., None]).reshape(b, L, G, Hg, P)
    da = (dt * a).reshape(b, L, G, Hg)

    def chunks(t):
        t = jnp.pad(t, [(0, 0), (0, pad)] + [(0, 0)] * (t.ndim - 2))
        return jnp.moveaxis(t.reshape((b, nc, Q) + t.shape[2:]), 1, 0)

    causal = jnp.tril(jnp.ones((Q, Q), dtype=bool))[None, :, :, None, None]

    def step(h, inp):
        xdt_c, da_c, b_c, c_c = inp
        acs = jnp.cumsum(da_c, axis=1)
        seg = acs[:, :, None] - acs[:, None, :]
        lmat = jnp.exp(jnp.where(causal, seg, -jnp.inf))
        cb = jnp.einsum('btgn,bsgn->btsg', c_c, b_c)
        y = jnp.einsum('btsgh,bsghp->btghp', cb[..., None] * lmat, xdt_c)
        y = y + jnp.einsum('btgn,bghpn->btghp', c_c, h) * jnp.exp(acs)[..., None]
        decay_to_end = jnp.exp(acs[:, -1:] - acs)
        h_new = h * jnp.exp(acs[:, -1])[..., None, None] + jnp.einsum(
            'bsgn,bsghp->bghpn', b_c, xdt_c * decay_to_end[..., None])
        return h_new, y

    h_init = h0.astype(f32).reshape(b, G, Hg, P, N)
    h_fin, y = lax.scan(step, h_init, (chunks(xdt), chunks(da), chunks(bm.astype(f32)), chunks(cm.astype(f32))))
    y = jnp.moveaxis(y, 0, 1).reshape(b, nc * Q, H, P)[:, :L]
    return y, h_fin.reshape(b, H, P, N)


def token_mixers(x, mem_k, mem_v, sc_prev, cv_prev, h_prev, w_in, w_sconv, w_sconv_out, w_ssd_conv, b_ssd_conv,
                 dt_bias, a_log, d_skip, norm_w, w_ssd_out, w_mem_out, w_o):
    b, L, _ = x.shape
    f32 = jnp.float32
    proj = x @ w_in
    sc_b, sc_c, sc_x, z, xbc, dt_raw, q, gate_logits = jnp.split(proj, _split_points(IN_SIZES), axis=-1)
    u = sc_c * sc_x
    v, sc_new = causal_depthwise_conv(u, sc_prev, w_sconv)
    y_a = (sc_b * v) @ w_sconv_out
    xbc_c, cv_new = causal_depthwise_conv(xbc, cv_prev, w_ssd_conv)
    xbc_c = jax.nn.silu(xbc_c + b_ssd_conv)
    xs, bm, cm = jnp.split(xbc_c, [SSD_INNER, SSD_INNER + SSD_GROUPS * SSD_STATE], axis=-1)
    xs = xs.reshape(b, L, SSD_HEADS, SSD_HEADDIM)
    bm = bm.reshape(b, L, SSD_GROUPS, SSD_STATE)
    cm = cm.reshape(b, L, SSD_GROUPS, SSD_STATE)
    dt = jax.nn.softplus(dt_raw.astype(f32) + dt_bias.astype(f32))
    a = -jnp.exp(a_log.astype(f32))
    y_s, h_new = ssd_chunked_scan(xs, dt, a, bm, cm, h_prev)
    y_s = y_s + d_skip.astype(f32)[:, None] * xs.astype(f32)
    y_s = gated_rmsnorm(y_s.reshape(b, L, SSD_INNER), z, norm_w).astype(x.dtype)
    y_b = y_s @ w_ssd_out
    qh = q.reshape(b, L, MEM_HEADS, MEM_HEAD_DIM)
    scores = jnp.einsum('blhd,bmhd->bhlm', qh, mem_k.astype(x.dtype)).astype(f32) * (MEM_HEAD_DIM ** -0.5)
    probs = jax.nn.softmax(scores, axis=-1).astype(x.dtype)
    o = jnp.einsum('bhlm,bmhd->blhd', probs, mem_v.astype(x.dtype)).reshape(b, L, MEM_DIM)
    y_m = o @ w_mem_out
    g = jax.nn.sigmoid(gate_logits.astype(f32)).astype(x.dtype).reshape(b, L, N_BRANCHES, D_MODEL)
    merged = g[:, :, 0] * y_a + g[:, :, 1] * y_b + g[:, :, 2] * y_m
    return merged @ w_o, sc_new, cv_new, h_new.astype(h_prev.dtype)


def hier_moe(x2, w_rg, b_rg, w_re, b_re, w_e_in, w_e_out):
    f32 = jnp.float32
    T = x2.shape[0]
    S = T * EXP_TOPK
    gp = jax.nn.softmax((x2 @ w_rg).astype(f32) + b_rg.astype(f32), axis=-1)
    g_w, g_idx = lax.top_k(gp, 1)
    g_w, g_idx = g_w[:, 0], g_idx[:, 0]
    elog = ((x2 @ w_re).astype(f32) + b_re.astype(f32)).reshape(T, N_EXP_GROUPS, EXP_PER_GROUP)
    ep = jax.nn.softmax(elog[jnp.arange(T), g_idx], axis=-1)
    e_w, e_idx = lax.top_k(ep, EXP_TOPK)
    gate = g_w[:, None] * e_w / jnp.sum(e_w, axis=-1, keepdims=True)
    expert = (g_idx[:, None] * EXP_PER_GROUP + e_idx).reshape(S).astype(jnp.int32)
    order = jnp.argsort(expert)
    counts = jnp.bincount(expert, length=N_EXPERTS)
    starts = jnp.cumsum(counts) - counts
    padded = (counts + MOE_BLOCK - 1) // MOE_BLOCK * MOE_BLOCK
    pends = jnp.cumsum(padded)
    pstarts = pends - padded
    se = expert[order]
    dest_sorted = (pstarts[se] + jnp.arange(S, dtype=jnp.int32) - starts[se]).astype(jnp.int32)
    dest = jnp.zeros((S,), jnp.int32).at[order].set(dest_sorted)
    n_blocks = (S + N_EXPERTS * (MOE_BLOCK - 1) + MOE_BLOCK - 1) // MOE_BLOCK
    buf = jnp.zeros((n_blocks * MOE_BLOCK, D_MODEL), x2.dtype).at[dest].set(jnp.repeat(x2, EXP_TOPK, axis=0))
    block_expert = jnp.minimum(
        jnp.searchsorted(pends, jnp.arange(n_blocks, dtype=jnp.int32) * MOE_BLOCK, side='right'), N_EXPERTS - 1)

    def expert_block(args):
        xb, e = args
        gu = xb @ w_e_in[e]
        gt, up = jnp.split(gu, 2, axis=-1)
        return (jax.nn.silu(gt) * up) @ w_e_out[e]

    out = lax.map(expert_block, (buf.reshape(n_blocks, MOE_BLOCK, D_MODEL), block_expert)).reshape(-1, D_MODEL)
    y = jnp.sum(out[dest].reshape(T, EXP_TOPK, D_MODEL).astype(f32) * gate[..., None], axis=1)
    return y.astype(x2.dtype)


def setup_inputs(seed: int = 0) -> dict:
    key = jax.random.key(seed)
    ks = iter(jax.random.split(key, 40))
    f32 = jnp.float32

    def nrm(shape, scale):
        return jax.random.normal(next(ks), shape, f32) * scale

    d = {}
    d["x_prompt"] = nrm((BATCH, SEQ, D_MODEL), 1.0)
    d["x_sample"] = nrm((DEC_BATCH, DEC_SEQ, D_MODEL), 1.0)
    d["mem_prompt"] = nrm((BATCH, MEM_LEN, D_MODEL), 1.0)
    d["cache_mem_k"] = nrm((DEPTH, DEC_BATCH, MEM_LEN, MEM_HEADS, MEM_HEAD_DIM), 1.0)
    d["cache_mem_v"] = nrm((DEPTH, DEC_BATCH, MEM_LEN, MEM_HEADS, MEM_HEAD_DIM), BETA)
    d["state_sconv"] = nrm((DEPTH, DEC_BATCH, SC_CONV_W - 1, SC_DIM), 1.0)
    d["state_ssd_conv"] = nrm((DEPTH, DEC_BATCH, SSD_CONV_W - 1, SSD_CONV_DIM), 1.0)
    d["state_ssd"] = nrm((DEPTH, DEC_BATCH, SSD_HEADS, SSD_HEADDIM, SSD_STATE), 0.1)
    d["w_in"] = nrm((DEPTH, D_MODEL, IN_PROJ_DIM), D_MODEL ** -0.5)
    d["w_sconv"] = nrm((DEPTH, SC_CONV_W, SC_DIM), SC_CONV_W ** -0.5)
    d["w_sconv_out"] = nrm((DEPTH, SC_DIM, D_MODEL), SC_DIM ** -0.5 * BETA)
    d["w_ssd_conv"] = nrm((DEPTH, SSD_CONV_W, SSD_CONV_DIM), SSD_CONV_W ** -0.5)
    d["b_ssd_conv"] = nrm((DEPTH, SSD_CONV_DIM), 0.02)
    u = jax.random.uniform(next(ks), (DEPTH, SSD_HEADS), f32)
    dt0 = jnp.exp(u * (math.log(0.1) - math.log(0.001)) + math.log(0.001))
    d["ssd_dt_bias"] = dt0 + jnp.log(-jnp.expm1(-dt0))
    d["ssd_a_log"] = jnp.log(jax.random.uniform(next(ks), (DEPTH, SSD_HEADS), f32, 1.0, 16.0))
    d["ssd_d"] = 1.0 + nrm((DEPTH, SSD_HEADS), 0.1)
    d["ssd_norm_w"] = 1.0 + nrm((DEPTH, SSD_INNER), 0.1)
    d["w_ssd_out"] = nrm((DEPTH, SSD_INNER, D_MODEL), SSD_INNER ** -0.5 * BETA)
    d["w_mem_k"] = nrm((DEPTH, D_MODEL, MEM_DIM), D_MODEL ** -0.5)
    d["w_mem_v"] = nrm((DEPTH, D_MODEL, MEM_DIM), D_MODEL ** -0.5 * BETA)
    d["w_mem_out"] = nrm((DEPTH, MEM_DIM, D_MODEL), MEM_DIM ** -0.5 * BETA)
    d["w_o"] = nrm((DEPTH, D_MODEL, D_MODEL), D_MODEL ** -0.5 * BETA)
    d["ln1_g"] = 1.0 + nrm((DEPTH, D_MODEL), 0.1)
    d["ln1_b"] = nrm((DEPTH, D_MODEL), 0.02)
    d["w_router_grp"] = nrm((DEPTH, D_MODEL, N_EXP_GROUPS), D_MODEL ** -0.5)
    d["b_router_grp"] = nrm((DEPTH, N_EXP_GROUPS), 0.01)
    d["w_router_exp"] = nrm((DEPTH, D_MODEL, N_EXPERTS), D_MODEL ** -0.5)
    d["b_router_exp"] = nrm((DEPTH, N_EXPERTS), 0.01)
    d["w_exp_in"] = nrm((DEPTH, N_EXPERTS, D_MODEL, 2 * EXP_FF), D_MODEL ** -0.5 * BETA)
    d["w_exp_out"] = nrm((DEPTH, N_EXPERTS, EXP_FF, D_MODEL), EXP_FF ** -0.5 * BETA)
    d["ln2_g"] = 1.0 + nrm((DEPTH, D_MODEL), 0.1)
    d["ln2_b"] = nrm((DEPTH, D_MODEL), 0.02)
    return d


def reference(x_prompt, x_sample, mem_prompt, cache_mem_k, cache_mem_v, state_sconv, state_ssd_conv, state_ssd,
              w_in, w_sconv, w_sconv_out, w_ssd_conv, b_ssd_conv, ssd_dt_bias, ssd_a_log, ssd_d, ssd_norm_w,
              w_ssd_out, w_mem_k, w_mem_v, w_mem_out, w_o, ln1_g, ln1_b, w_router_grp, b_router_grp,
              w_router_exp, b_router_exp, w_exp_in, w_exp_out, ln2_g, ln2_b):

    def decoder_layer(l, x, mem_k, mem_v, sc_prev, cv_prev, h_prev):
        mix, sc_new, cv_new, h_new = token_mixers(
            x, mem_k, mem_v, sc_prev, cv_prev, h_prev, w_in[l], w_sconv[l], w_sconv_out[l], w_ssd_conv[l],
            b_ssd_conv[l], ssd_dt_bias[l], ssd_a_log[l], ssd_d[l], ssd_norm_w[l], w_ssd_out[l], w_mem_out[l], w_o[l])
        x = layer_norm(ALPHA * x + mix, ln1_g[l], ln1_b[l])
        b, L, _ = x.shape
        ffn = hier_moe(x.reshape(b * L, D_MODEL), w_router_grp[l], b_router_grp[l], w_router_exp[l],
                       b_router_exp[l], w_exp_in[l], w_exp_out[l]).reshape(b, L, D_MODEL)
        x = layer_norm(ALPHA * x + ffn, ln2_g[l], ln2_b[l])
        return x, sc_new, cv_new, h_new

    bp = x_prompt.shape[0]
    yp = x_prompt
    mk_p, mv_p, sc_p, cv_p, h_p = [], [], [], [], []
    for l in range(DEPTH):
        mk = (mem_prompt @ w_mem_k[l]).reshape(bp, MEM_LEN, MEM_HEADS, MEM_HEAD_DIM)
        mv = (mem_prompt @ w_mem_v[l]).reshape(bp, MEM_LEN, MEM_HEADS, MEM_HEAD_DIM)
        sc0 = jnp.zeros((bp, SC_CONV_W - 1, SC_DIM), x_prompt.dtype)
        cv0 = jnp.zeros((bp, SSD_CONV_W - 1, SSD_CONV_DIM), x_prompt.dtype)
        h0 = jnp.zeros((bp, SSD_HEADS, SSD_HEADDIM, SSD_STATE), x_prompt.dtype)
        yp, sc_n, cv_n, h_n = decoder_layer(l, yp, mk, mv, sc0, cv0, h0)
        mk_p.append(mk)
        mv_p.append(mv)
        sc_p.append(sc_n)
        cv_p.append(cv_n)
        h_p.append(h_n)

    ys = x_sample
    sc_s, cv_s, h_s = [], [], []
    for l in range(DEPTH):
        ys, sc_n, cv_n, h_n = decoder_layer(l, ys, cache_mem_k[l], cache_mem_v[l], state_sconv[l],
                                            state_ssd_conv[l], state_ssd[l])
        sc_s.append(sc_n)
        cv_s.append(cv_n)
        h_s.append(h_n)

    return (yp, ys, jnp.stack(mk_p), jnp.stack(mv_p), jnp.stack(sc_p), jnp.stack(cv_p), jnp.stack(h_p),
            jnp.stack(sc_s), jnp.stack(cv_s), jnp.stack(h_s))
```

```python
import functools

import jax
import jax.numpy as jnp
from jax import lax
from jax.experimental import pallas as pl
from jax.experimental.pallas import tpu as pltpu

F32 = jnp.float32
BF16 = jnp.bfloat16

D_MODEL = 1024
DEPTH = 4
SC_DIM = D_MODEL
SC_CONV_W = 3
SSD_INNER = 2 * D_MODEL
SSD_HEADDIM = 64
SSD_HEADS = SSD_INNER // SSD_HEADDIM
SSD_GROUPS = 4
SSD_STATE = 128
SSD_CONV_W = 4
SSD_BC = SSD_GROUPS * SSD_STATE
SSD_CONV_DIM = SSD_INNER + 2 * SSD_BC
SSD_CHUNK = 128
HEADS_PER_GROUP = SSD_HEADS // SSD_GROUPS
GROUP_INNER = SSD_INNER // SSD_GROUPS
MEM_LEN = 256
MEM_HEADS = 4
MEM_HEAD_DIM = D_MODEL // MEM_HEADS
N_EXP_GROUPS = 4
EXP_PER_GROUP = 8
N_EXPERTS = N_EXP_GROUPS * EXP_PER_GROUP
EXP_TOPK = 2
EXP_FF = 512
MOE_BLOCK = 128
ALPHA = (2 * DEPTH) ** 0.25
LN_EPS = 1e-5
RMS_EPS = 1e-6

LANES = 128
SUBLANES = 8
VMEM_LIMIT = 56 * 1024 * 1024

COL_XBC = 0
COL_BCX = SSD_CONV_DIM
COL_Z = COL_BCX + 3 * SC_DIM
COL_Q = COL_Z + SSD_INNER
COL_G = COL_Q + D_MODEL
PROJ_COLS = COL_G + 3 * D_MODEL


def _dot(a, b):
    return jnp.dot(a, b, preferred_element_type=F32)


def _dot_exact(a, b):
    return jnp.dot(a, b, preferred_element_type=F32, precision=lax.Precision.HIGHEST)


def _sigmoid(x):
    return 1.0 / (1.0 + jnp.exp(-x))


def _silu(x):
    return x * _sigmoid(x)


def _layer_norm(x, g, b):
    mu = jnp.mean(x, axis=-1, keepdims=True)
    xc = x - mu
    var = jnp.mean(xc * xc, axis=-1, keepdims=True)
    return xc * lax.rsqrt(var + LN_EPS) * g + b


def _params(*sem):
    return pltpu.CompilerParams(dimension_semantics=sem, vmem_limit_bytes=VMEM_LIMIT)


def _mm_kernel(x_ref, w_ref, o_ref):
    o_ref[...] = _dot(x_ref[...].astype(BF16), w_ref[...])


def _matmul(x, w, tm, tn):
    m, k = x.shape
    n = w.shape[1]
    tm = min(tm, m)
    tn = min(tn, n)
    return pl.pallas_call(
        _mm_kernel,
        grid=(m // tm, n // tn),
        in_specs=[pl.BlockSpec((tm, k), lambda i, j: (i, 0)),
                  pl.BlockSpec((k, tn), lambda i, j: (0, j))],
        out_specs=pl.BlockSpec((tm, tn), lambda i, j: (i, j)),
        out_shape=jax.ShapeDtypeStruct((m, n), F32),
        compiler_params=_params("parallel", "arbitrary"),
        name="matmul",
    )(x, w)


def _branch_a_kernel(bcx_ref, g_ref, prev_ref, wc_ref, wout_ref, o_ref, st_ref, ext_ref, *, nseq, lt, valid):
    c_dim = SC_DIM
    halo = SC_CONV_W - 1
    base = SUBLANES
    rows = nseq * lt
    bcx = bcx_ref[...]
    scb = bcx[:, :c_dim]
    u = (bcx[:, c_dim:2 * c_dim] * bcx[:, 2 * c_dim:]).reshape(nseq, lt, c_dim)

    @pl.when(pl.program_id(1) == 0)
    def _():
        ext_ref[:, base - halo:base, :] = prev_ref[...]

    ext_ref[:, base:base + lt, :] = u
    w = wc_ref[...]
    v = ext_ref[:, base - 2:base - 2 + lt, :] * w[0:1, :]
    v = v + ext_ref[:, base - 1:base - 1 + lt, :] * w[1:2, :]
    v = v + u * w[2:3, :]
    st_ref[...] = ext_ref[:, base + valid - halo:base + valid, :]
    ext_ref[:, base - halo:base, :] = ext_ref[:, base + lt - halo:base + lt, :]
    ya = _dot((scb * v.reshape(rows, c_dim)).astype(BF16), wout_ref[...])
    o_ref[...] = _sigmoid(g_ref[...]) * ya


def _branch_a(proj, prev, w_conv, w_out, *, nb, ls, nseq, lt, valid):
    nj = ls // lt
    rows = nseq * lt
    kern = functools.partial(_branch_a_kernel, nseq=nseq, lt=lt, valid=valid)
    return pl.pallas_call(
        kern,
        grid=(nb // nseq, nj),
        in_specs=[pl.BlockSpec((rows, 3 * SC_DIM), lambda i, j: (i * nj + j, COL_BCX // (3 * SC_DIM))),
                  pl.BlockSpec((rows, D_MODEL), lambda i, j: (i * nj + j, COL_G // D_MODEL)),
                  pl.BlockSpec((nseq, SC_CONV_W - 1, SC_DIM), lambda i, j: (i, 0, 0)),
                  pl.BlockSpec((SC_CONV_W, SC_DIM), lambda i, j: (0, 0)),
                  pl.BlockSpec((SC_DIM, D_MODEL), lambda i, j: (0, 0))],
        out_specs=[pl.BlockSpec((rows, D_MODEL), lambda i, j: (i * nj + j, 0)),
                   pl.BlockSpec((nseq, SC_CONV_W - 1, SC_DIM), lambda i, j: (i, 0, 0))],
        out_shape=[jax.ShapeDtypeStruct((nb * ls, D_MODEL), F32),
                   jax.ShapeDtypeStruct((nb, SC_CONV_W - 1, SC_DIM), F32)],
        scratch_shapes=[pltpu.VMEM((nseq, SUBLANES + lt, SC_DIM), F32)],
        compiler_params=_params("arbitrary", "arbitrary"),
        name="branch_a",
    )(proj, proj, prev, w_conv, w_out)


def _branch_m_kernel(q_ref, g_ref, k_ref, v_ref, wout_ref, o_ref, *, nseq, lt):
    rows = nseq * lt
    scale = MEM_HEAD_DIM ** -0.5
    qb = q_ref[...].astype(BF16).reshape(nseq, lt, D_MODEL)
    outs = []
    for h in range(MEM_HEADS):
        sl = slice(h * MEM_HEAD_DIM, (h + 1) * MEM_HEAD_DIM)
        kh = k_ref[:, :, sl].astype(BF16)
        vh = v_ref[:, :, sl].astype(BF16)
        s = jnp.einsum("sld,smd->slm", qb[:, :, sl], kh, preferred_element_type=F32) * scale
        s = s - jnp.max(s, axis=-1, keepdims=True)
        p = jnp.exp(s)
        p = p / jnp.sum(p, axis=-1, keepdims=True)
        outs.append(jnp.einsum("slm,smd->sld", p.astype(BF16), vh, preferred_element_type=F32))
    o = jnp.concatenate(outs, axis=-1).reshape(rows, D_MODEL)
    o_ref[...] = _sigmoid(g_ref[...]) * _dot(o.astype(BF16), wout_ref[...])


def _branch_m(proj, mem_k, mem_v, w_out, *, nb, ls, nseq, lt):
    nj = ls // lt
    rows = nseq * lt
    kern = functools.partial(_branch_m_kernel, nseq=nseq, lt=lt)
    return pl.pallas_call(
        kern,
        grid=(nb // nseq, nj),
        in_specs=[pl.BlockSpec((rows, D_MODEL), lambda i, j: (i * nj + j, COL_Q // D_MODEL)),
                  pl.BlockSpec((rows, D_MODEL), lambda i, j: (i * nj + j, COL_G // D_MODEL + 2)),
                  pl.BlockSpec((nseq, MEM_LEN, D_MODEL), lambda i, j: (i, 0, 0)),
                  pl.BlockSpec((nseq, MEM_LEN, D_MODEL), lambda i, j: (i, 0, 0)),
                  pl.BlockSpec((D_MODEL, D_MODEL), lambda i, j: (0, 0))],
        out_specs=pl.BlockSpec((rows, D_MODEL), lambda i, j: (i * nj + j, 0)),
        out_shape=jax.ShapeDtypeStruct((nb * ls, D_MODEL), F32),
        compiler_params=_params("parallel", "arbitrary"),
        name="branch_m",
    )(proj, proj, mem_k, mem_v, w_out)


def _softplus(x):
    return jnp.maximum(x, 0.0) + jnp.log1p(jnp.exp(-jnp.abs(x)))


def _ssd_chunk(s, c, xc_ref, z_ref, dt_ref, h_ref, yn_ref, a_row, d_row, nw_row, expand, *, lt, q):
    r0 = pl.multiple_of(c * q, q)
    row0 = pl.multiple_of(s * lt + c * q, q)
    xs = xc_ref[s, pl.ds(r0, q), 0:SSD_INNER]
    bm = xc_ref[s, pl.ds(r0, q), SSD_INNER:SSD_INNER + SSD_BC].astype(BF16)
    cm = xc_ref[s, pl.ds(r0, q), SSD_INNER + SSD_BC:SSD_CONV_DIM].astype(BF16)
    dt = dt_ref[pl.ds(row0, q), :]
    da = dt * a_row
    ti = lax.broadcasted_iota(jnp.int32, (q, q), 0)
    si = lax.broadcasted_iota(jnp.int32, (q, q), 1)
    causal = si <= ti
    tri = causal.astype(F32)
    acs = _dot_exact(tri, da)
    acs_t = lax.dot_general(da, tri, (((0,), (1,)), ((), ())), precision=lax.Precision.HIGHEST,
                            preferred_element_type=F32)
    acs_last = acs[q - 1:q, :]
    stack = jnp.concatenate([dt, jnp.exp(acs), jnp.exp(acs_last - acs)], axis=0)
    stack_e = _dot_exact(stack, expand)
    dt_e, eacs_e, dte_e = stack_e[0:q], stack_e[q:2 * q], stack_e[2 * q:3 * q]
    xdt = xs * dt_e
    xdt_b = xdt.astype(BF16)
    xdte_b = (xdt * dte_e).astype(BF16)
    dec_full = jnp.broadcast_to(jnp.exp(acs_t[:, q - 1:q]), (LANES, SSD_STATE))
    lane = lax.broadcasted_iota(jnp.int32, (q, LANES), 1)
    lo_mask = lane < SSD_HEADDIM
    y_groups = []
    for g in range(SSD_GROUPS):
        gs = slice(g * SSD_STATE, (g + 1) * SSD_STATE)
        gi = slice(g * GROUP_INNER, (g + 1) * GROUP_INNER)
        cb = lax.dot_general(cm[:, gs], bm[:, gs], (((1,), (1,)), ((), ())), preferred_element_type=F32)
        h0 = g * HEADS_PER_GROUP
        hg = h_ref[s, h0:h0 + HEADS_PER_GROUP].reshape(GROUP_INNER, SSD_STATE)
        y_int = lax.dot_general(cm[:, gs], hg.astype(BF16), (((1,), (1,)), ((), ())),
                                preferred_element_type=F32) * eacs_e[:, gi]
        upd = lax.dot_general(xdte_b[:, gi], bm[:, gs], (((0,), (0,)), ((), ())), preferred_element_type=F32)
        y_pairs = []
        for pr in range(HEADS_PER_GROUP // 2):
            col = g * GROUP_INNER + pr * LANES
            xpair = xdt_b[:, col:col + LANES]
            acc = None
            for hh in range(2):
                h = h0 + 2 * pr + hh
                seg = acs[:, h:h + 1] - acs_t[h:h + 1, :]
                lm = jnp.exp(jnp.where(causal, seg, -jnp.inf))
                m = (cb * lm).astype(BF16)
                keep = lo_mask if hh == 0 else jnp.logical_not(lo_mask)
                part = _dot(m, jnp.where(keep, xpair, jnp.zeros_like(xpair)))
                acc = part if acc is None else acc + part
            y_pairs.append(acc)
        y_groups.append(jnp.concatenate(y_pairs, axis=-1) + y_int)
        for hh in range(HEADS_PER_GROUP):
            h = h0 + hh
            h_ref[s, h] = h_ref[s, h] * dec_full[h:h + 1, :] + upd[hh * SSD_HEADDIM:(hh + 1) * SSD_HEADDIM, :]
    y = jnp.concatenate(y_groups, axis=-1) + d_row * xs
    yf = y * _silu(z_ref[pl.ds(row0, q), :])
    normed = []
    for g in range(SSD_GROUPS):
        yg = yf[:, g * GROUP_INNER:(g + 1) * GROUP_INNER]
        normed.append(yg * lax.rsqrt(jnp.mean(yg * yg, axis=-1, keepdims=True) + RMS_EPS))
    yn_ref[pl.ds(row0, q), :] = jnp.concatenate(normed, axis=-1) * nw_row


def _branch_b_kernel(xbc_ref, z_ref, dtr_ref, g_ref, cprev_ref, h0_ref, wc_ref, bc_ref, dtb_ref, alog_ref,
                     d_ref, nw_ref, exp_ref, wout_ref, o_ref, cst_ref, h_ref, ext_ref, xc_ref, dt_ref, yn_ref,
                     *, nseq, lt, q, valid):
    halo = SSD_CONV_W - 1
    base = SUBLANES
    rows = nseq * lt

    @pl.when(pl.program_id(1) == 0)
    def _():
        ext_ref[:, base - halo:base, :] = cprev_ref[...]
        h_ref[...] = h0_ref[...]

    ext_ref[:, base:base + lt, :] = xbc_ref[...].reshape(nseq, lt, SSD_CONV_DIM)
    w = wc_ref[...]
    conv = ext_ref[:, base - 3:base - 3 + lt, :] * w[0:1, :]
    for k in range(1, SSD_CONV_W):
        conv = conv + ext_ref[:, base - 3 + k:base - 3 + k + lt, :] * w[k:k + 1, :]
    cst_ref[...] = ext_ref[:, base + valid - halo:base + valid, :]
    ext_ref[:, base - halo:base, :] = ext_ref[:, base + lt - halo:base + lt, :]
    xc_ref[...] = _silu(conv + bc_ref[...])
    dt = _softplus(dtr_ref[...] + dtb_ref[...])
    if valid < lt:
        t_in_seq = lax.broadcasted_iota(jnp.int32, (nseq, lt, LANES), 1).reshape(rows, LANES)
        dt = jnp.where(t_in_seq < valid, dt, 0.0)
    dt_ref[...] = dt
    a_row = -jnp.exp(alog_ref[...])
    d_row = d_ref[...]
    nw_row = nw_ref[...]
    expand = exp_ref[...]
    nchunk = lt // q

    def body(n, carry):
        _ssd_chunk(n // nchunk, n % nchunk, xc_ref, z_ref, dt_ref, h_ref, yn_ref, a_row, d_row, nw_row, expand,
                   lt=lt, q=q)
        return carry

    lax.fori_loop(0, nseq * nchunk, body, 0)
    o_ref[...] = _sigmoid(g_ref[...]) * _dot(yn_ref[...].astype(BF16), wout_ref[...])


def _branch_b(proj, dtr, cprev, h0, w_conv, b_conv, dt_bias, a_log, d_exp, norm_w, expand, w_out,
              *, nb, ls, nseq, lt, q, valid):
    nj = ls // lt
    rows = nseq * lt
    kern = functools.partial(_branch_b_kernel, nseq=nseq, lt=lt, q=q, valid=valid)
    row_blk = lambda i, j: (i * nj + j, 0)
    const2 = lambda i, j: (0, 0)
    return pl.pallas_call(
        kern,
        grid=(nb // nseq, nj),
        in_specs=[pl.BlockSpec((rows, SSD_CONV_DIM), lambda i, j: (i * nj + j, COL_XBC // SSD_CONV_DIM)),
                  pl.BlockSpec((rows, SSD_INNER), lambda i, j: (i * nj + j, COL_Z // SSD_INNER)),
                  pl.BlockSpec((rows, LANES), row_blk),
                  pl.BlockSpec((rows, D_MODEL), lambda i, j: (i * nj + j, COL_G // D_MODEL + 1)),
                  pl.BlockSpec((nseq, SSD_CONV_W - 1, SSD_CONV_DIM), lambda i, j: (i, 0, 0)),
                  pl.BlockSpec((nseq, SSD_HEADS, SSD_HEADDIM, SSD_STATE), lambda i, j: (i, 0, 0, 0)),
                  pl.BlockSpec((SSD_CONV_W, SSD_CONV_DIM), const2),
                  pl.BlockSpec((1, SSD_CONV_DIM), const2),
                  pl.BlockSpec((1, LANES), const2),
                  pl.BlockSpec((1, LANES), const2),
                  pl.BlockSpec((1, SSD_INNER), const2),
                  pl.BlockSpec((1, SSD_INNER), const2),
                  pl.BlockSpec((LANES, SSD_INNER), const2),
                  pl.BlockSpec((SSD_INNER, D_MODEL), const2)],
        out_specs=[pl.BlockSpec((rows, D_MODEL), row_blk),
                   pl.BlockSpec((nseq, SSD_CONV_W - 1, SSD_CONV_DIM), lambda i, j: (i, 0, 0)),
                   pl.BlockSpec((nseq, SSD_HEADS, SSD_HEADDIM, SSD_STATE), lambda i, j: (i, 0, 0, 0))],
        out_shape=[jax.ShapeDtypeStruct((nb * ls, D_MODEL), F32),
                   jax.ShapeDtypeStruct((nb, SSD_CONV_W - 1, SSD_CONV_DIM), F32),
                   jax.ShapeDtypeStruct((nb, SSD_HEADS, SSD_HEADDIM, SSD_STATE), F32)],
        scratch_shapes=[pltpu.VMEM((nseq, SUBLANES + lt, SSD_CONV_DIM), F32),
                        pltpu.VMEM((nseq, lt, SSD_CONV_DIM), F32),
                        pltpu.VMEM((rows, LANES), F32),
                        pltpu.VMEM((rows, SSD_INNER), F32)],
        compiler_params=_params("arbitrary", "arbitrary"),
        name="branch_b",
    )(proj, proj, dtr, proj, cprev, h0, w_conv, b_conv, dt_bias, a_log, d_exp, norm_w, expand, w_out)


def _merge_kernel(ya_ref, yb_ref, ym_ref, x_ref, wo_ref, g_ref, b_ref, wr_ref, br_ref, x1_ref, route_ref):
    merged = (ya_ref[...] + yb_ref[...]) + ym_ref[...]
    mix = _dot(merged.astype(BF16), wo_ref[...])
    x1 = _layer_norm(ALPHA * x_ref[...] + mix, g_ref[...], b_ref[...])
    x1_ref[...] = x1
    logits = _dot_exact(x1, wr_ref[...]) + br_ref[...]
    lane = lax.broadcasted_iota(jnp.int32, logits.shape, 1).astype(F32)
    neg = -jnp.inf
    big = float(LANES)
    is_grp = lane < N_EXP_GROUPS
    gl = jnp.where(is_grp, logits, neg)
    gmax = jnp.max(gl, axis=-1, keepdims=True)
    gsum = jnp.sum(jnp.exp(gl - gmax), axis=-1, keepdims=True)
    g_idx = jnp.min(jnp.where(jnp.logical_and(is_grp, gl == gmax), lane, big), axis=-1, keepdims=True)
    g_w = 1.0 / gsum
    e_lo = N_EXP_GROUPS + g_idx * EXP_PER_GROUP
    in_grp = jnp.logical_and(lane >= e_lo, lane < e_lo + EXP_PER_GROUP)
    el = jnp.where(in_grp, logits, neg)
    emax = jnp.max(el, axis=-1, keepdims=True)
    ee = jnp.exp(el - emax)
    esum = jnp.sum(ee, axis=-1, keepdims=True)
    i0 = jnp.min(jnp.where(jnp.logical_and(in_grp, el == emax), lane, big), axis=-1, keepdims=True)
    el2 = jnp.where(lane == i0, neg, el)
    emax2 = jnp.max(el2, axis=-1, keepdims=True)
    i1 = jnp.min(jnp.where(jnp.logical_and(in_grp, el2 == emax2), lane, big), axis=-1, keepdims=True)
    w0 = 1.0 / esum
    w1 = jnp.exp(emax2 - emax) / esum
    wsum = w0 + w1
    gate0 = g_w * w0 / wsum
    gate1 = g_w * w1 / wsum
    id0 = i0 - N_EXP_GROUPS
    id1 = i1 - N_EXP_GROUPS
    route = jnp.where(lane == 0, id0, jnp.where(lane == 1, id1, jnp.where(lane == 2, gate0,
                      jnp.where(lane == 3, gate1, 0.0))))
    route_ref[...] = route


def _merge(ya, yb, ym, x, w_o, ln_g, ln_b, w_r, b_r, *, tm):
    t = x.shape[0]
    row = lambda i: (i, 0)
    const = lambda i: (0, 0)
    return pl.pallas_call(
        _merge_kernel,
        grid=(t // tm,),
        in_specs=[pl.BlockSpec((tm, D_MODEL), row)] * 4 + [
            pl.BlockSpec((D_MODEL, D_MODEL), const),
            pl.BlockSpec((1, D_MODEL), const), pl.BlockSpec((1, D_MODEL), const),
            pl.BlockSpec((D_MODEL, LANES), const), pl.BlockSpec((1, LANES), const)],
        out_specs=[pl.BlockSpec((tm, D_MODEL), row), pl.BlockSpec((tm, LANES), row)],
        out_shape=[jax.ShapeDtypeStruct((t, D_MODEL), F32), jax.ShapeDtypeStruct((t, LANES), F32)],
        compiler_params=_params("parallel"),
        name="merge_ln_router",
    )(ya, yb, ym, x, w_o, ln_g, ln_b, w_r, b_r)


def _moe_kernel(bexp_ref, nused_ref, src_ref, gate_ref, x_ref, win_ref, wout_ref, g_ref, b_ref, o_ref,
                xg_ref, eo_ref, *, nblk):
    c = pl.program_id(0)
    i = pl.program_id(1)

    @pl.when(i == 0)
    def _():
        o_ref[...] = jnp.zeros_like(o_ref)

    @pl.when(i < nused_ref[c])
    def _():
        def gather(r, carry):
            tok = src_ref[0, 0, i * MOE_BLOCK + r]
            xg_ref[pl.ds(r, 1), :] = x_ref[pl.ds(tok, 1), :]
            return carry

        lax.fori_loop(0, MOE_BLOCK, gather, 0)
        gu = _dot(xg_ref[...].astype(BF16), win_ref[0])
        hid = _silu(gu[:, :EXP_FF]) * gu[:, EXP_FF:]
        eo_ref[...] = _dot(hid.astype(BF16), wout_ref[0])

        def scatter(r, carry):
            tok = src_ref[0, 0, i * MOE_BLOCK + r]
            gate = gate_ref[0, 0, i * MOE_BLOCK + r]
            o_ref[pl.ds(tok, 1), :] = o_ref[pl.ds(tok, 1), :] + gate * eo_ref[pl.ds(r, 1), :]
            return carry

        lax.fori_loop(0, MOE_BLOCK, scatter, 0)

    @pl.when(i == nblk - 1)
    def _():
        o_ref[...] = _layer_norm(ALPHA * x_ref[...] + o_ref[...], g_ref[...], b_ref[...])


def _moe(x1, bexp, nused, src, gate, w_in, w_out, ln_g, ln_b, *, layer, tc, nblk):
    t = x1.shape[0]
    nchunks = t // tc
    kern = functools.partial(_moe_kernel, nblk=nblk)
    smem_row = lambda c, i, be, nu: (c, 0, 0)
    grid_spec = pltpu.PrefetchScalarGridSpec(
        num_scalar_prefetch=2,
        grid=(nchunks, nblk),
        in_specs=[pl.BlockSpec((1, 1, nblk * MOE_BLOCK), smem_row, memory_space=pltpu.SMEM),
                  pl.BlockSpec((1, 1, nblk * MOE_BLOCK), smem_row, memory_space=pltpu.SMEM),
                  pl.BlockSpec((tc, D_MODEL), lambda c, i, be, nu: (c, 0), pipeline_mode=pl.Buffered(1)),
                  pl.BlockSpec((1, D_MODEL, 2 * EXP_FF),
                               lambda c, i, be, nu: (layer * N_EXPERTS + be[c * nblk + i], 0, 0)),
                  pl.BlockSpec((1, EXP_FF, D_MODEL),
                               lambda c, i, be, nu: (layer * N_EXPERTS + be[c * nblk + i], 0, 0)),
                  pl.BlockSpec((1, D_MODEL), lambda c, i, be, nu: (0, 0)),
                  pl.BlockSpec((1, D_MODEL), lambda c, i, be, nu: (0, 0))],
        out_specs=pl.BlockSpec((tc, D_MODEL), lambda c, i, be, nu: (c, 0)),
        scratch_shapes=[pltpu.VMEM((MOE_BLOCK, D_MODEL), F32), pltpu.VMEM((MOE_BLOCK, D_MODEL), F32)],
    )
    return pl.pallas_call(
        kern,
        grid_spec=grid_spec,
        out_shape=jax.ShapeDtypeStruct((t, D_MODEL), F32),
        compiler_params=_params("arbitrary", "arbitrary"),
        name="moe_experts",
    )(bexp, nused, src[:, None, :], gate[:, None, :], x1, w_in, w_out, ln_g, ln_b)


def _dispatch(route, *, tc, nblk):
    t = route.shape[0]
    nchunks = t // tc
    s = tc * EXP_TOPK
    ids = route[:, 0:EXP_TOPK].astype(jnp.int32).reshape(nchunks, s)
    gates = route[:, EXP_TOPK:2 * EXP_TOPK].reshape(nchunks, s)
    onehot = (ids[..., None] == jnp.arange(N_EXPERTS, dtype=jnp.int32)).astype(jnp.int32)
    csum = jnp.cumsum(onehot, axis=1)
    rank = jnp.sum((csum - onehot) * onehot, axis=-1)
    counts = csum[:, -1, :]
    padded = (counts + MOE_BLOCK - 1) // MOE_BLOCK * MOE_BLOCK
    pends = jnp.cumsum(padded, axis=-1)
    pstarts = pends - padded
    dest = jnp.take_along_axis(pstarts, ids, axis=1) + rank
    cidx = jnp.arange(nchunks, dtype=jnp.int32)[:, None]
    tok = jnp.broadcast_to(jnp.arange(s, dtype=jnp.int32)[None, :] // EXP_TOPK, (nchunks, s))
    src = jnp.zeros((nchunks, nblk * MOE_BLOCK), jnp.int32).at[cidx, dest].set(tok)
    gate_rows = jnp.zeros((nchunks, nblk * MOE_BLOCK), F32).at[cidx, dest].set(gates)
    blk_start = jnp.arange(nblk, dtype=jnp.int32) * MOE_BLOCK
    bexp = jnp.sum(pends[:, None, :] <= blk_start[None, :, None], axis=-1)
    bexp = jnp.minimum(bexp, N_EXPERTS - 1).astype(jnp.int32).reshape(-1)
    nused = (pends[:, -1] // MOE_BLOCK).astype(jnp.int32)
    return bexp, nused, src, gate_rows


def _prep_weights(w_in, w_sconv_out, w_ssd_out, w_mem_k, w_mem_v, w_mem_out, w_o, w_exp_in, w_exp_out,
                  w_router_grp, b_router_grp, w_router_exp, b_router_exp, ssd_dt_bias, ssd_a_log, ssd_d):
    o_b, o_c, o_x, o_z, o_xbc, o_dt, o_q, o_g = 0, 1024, 2048, 3072, 5120, 8192, 8224, 9248
    w_main = jnp.concatenate([w_in[:, :, o_xbc:o_dt], w_in[:, :, o_b:o_z], w_in[:, :, o_z:o_xbc],
                              w_in[:, :, o_q:o_g], w_in[:, :, o_g:]], axis=-1).astype(BF16)
    pad_h = LANES - SSD_HEADS
    w_dt = jnp.pad(w_in[:, :, o_dt:o_q], ((0, 0), (0, 0), (0, pad_h))).astype(BF16)
    w_r = jnp.pad(jnp.concatenate([w_router_grp, w_router_exp], axis=-1),
                  ((0, 0), (0, 0), (0, LANES - N_EXP_GROUPS - N_EXPERTS)))
    b_r = jnp.pad(jnp.concatenate([b_router_grp, b_router_exp], axis=-1),
                  ((0, 0), (0, LANES - N_EXP_GROUPS - N_EXPERTS)))[:, None, :]
    head_of_col = jnp.arange(SSD_INNER, dtype=jnp.int32) // SSD_HEADDIM
    expand = (jnp.arange(LANES, dtype=jnp.int32)[:, None] == head_of_col[None, :]).astype(F32)
    return dict(
        w_main=w_main, w_dt=w_dt, w_r=w_r, b_r=b_r, expand=expand,
        w_sconv_out=w_sconv_out.astype(BF16), w_ssd_out=w_ssd_out.astype(BF16),
        w_mem_kv=jnp.concatenate([w_mem_k, w_mem_v], axis=-1).astype(BF16),
        w_mem_out=w_mem_out.astype(BF16), w_o=w_o.astype(BF16),
        w_exp_in=w_exp_in.astype(BF16).reshape(DEPTH * N_EXPERTS, D_MODEL, 2 * EXP_FF),
        w_exp_out=w_exp_out.astype(BF16).reshape(DEPTH * N_EXPERTS, EXP_FF, D_MODEL),
        dt_bias=jnp.pad(ssd_dt_bias, ((0, 0), (0, pad_h)))[:, None, :],
        a_log=jnp.pad(ssd_a_log, ((0, 0), (0, pad_h)))[:, None, :],
        d_exp=jnp.repeat(ssd_d, SSD_HEADDIM, axis=-1)[:, None, :],
    )


def _moe_blocks(tc):
    return (tc * EXP_TOPK + N_EXPERTS * (MOE_BLOCK - 1) + MOE_BLOCK - 1) // MOE_BLOCK


def _decoder_layer(l, x, mem_k, mem_v, sc_prev, cv_prev, h_prev, wp, p, cfg):
    nb, ls = cfg["nb"], cfg["ls"]
    proj = _matmul(x, wp["w_main"][l], cfg["tm_proj"], 1024)
    dtr = _matmul(x, wp["w_dt"][l], cfg["tm_proj"], LANES)
    ya, sc_new = _branch_a(proj, sc_prev, p["w_sconv"][l], wp["w_sconv_out"][l],
                           nb=nb, ls=ls, nseq=cfg["a_nseq"], lt=cfg["a_lt"], valid=cfg["a_valid"])
    yb, cv_new, h_new = _branch_b(proj, dtr, cv_prev, h_prev, p["w_ssd_conv"][l], p["b_ssd_conv"][l][None, :],
                                  wp["dt_bias"][l], wp["a_log"][l], wp["d_exp"][l], p["ssd_norm_w"][l][None, :],
                                  wp["expand"], wp["w_ssd_out"][l],
                                  nb=nb, ls=ls, nseq=cfg["b_nseq"], lt=cfg["b_lt"], q=cfg["b_q"],
                                  valid=cfg["b_valid"])
    ym = _branch_m(proj, mem_k, mem_v, wp["w_mem_out"][l], nb=nb, ls=ls, nseq=cfg["m_nseq"], lt=cfg["m_lt"])
    x1, route = _merge(ya, yb, ym, x, wp["w_o"][l], p["ln1_g"][l][None, :], p["ln1_b"][l][None, :],
                       wp["w_r"][l], wp["b_r"][l], tm=cfg["tm_merge"])
    tc = cfg["moe_tc"]
    nblk = _moe_blocks(tc)
    bexp, nused, src, gate_rows = _dispatch(route, tc=tc, nblk=nblk)
    x2 = _moe(x1, bexp, nused, src, gate_rows, wp["w_exp_in"], wp["w_exp_out"],
              p["ln2_g"][l][None, :], p["ln2_b"][l][None, :], layer=l, tc=tc, nblk=nblk)
    return x2, sc_new, cv_new, h_new


PROMPT_CFG = dict(tm_proj=1024, a_nseq=1, a_lt=512, b_nseq=1, b_lt=256, b_q=SSD_CHUNK, m_nseq=1, m_lt=512,
                  tm_merge=512, moe_tc=2048)
SAMPLE_PAD = SUBLANES


def kernel(x_prompt, x_sample, mem_prompt, cache_mem_k, cache_mem_v, state_sconv, state_ssd_conv, state_ssd, w_in, w_sconv, w_sconv_out, w_ssd_conv, b_ssd_conv, ssd_dt_bias, ssd_a_log, ssd_d, ssd_norm_w, w_ssd_out, w_mem_k, w_mem_v, w_mem_out, w_o, ln1_g, ln1_b, w_router_grp, b_router_grp, w_router_exp, b_router_exp, w_exp_in, w_exp_out, ln2_g, ln2_b):
    bp, seq, _ = x_prompt.shape
    bs, dec_seq, _ = x_sample.shape
    wp = _prep_weights(w_in, w_sconv_out, w_ssd_out, w_mem_k, w_mem_v, w_mem_out, w_o, w_exp_in, w_exp_out,
                       w_router_grp, b_router_grp, w_router_exp, b_router_exp, ssd_dt_bias, ssd_a_log, ssd_d)
    p = dict(w_sconv=w_sconv, w_ssd_conv=w_ssd_conv, b_ssd_conv=b_ssd_conv, ssd_norm_w=ssd_norm_w,
             ln1_g=ln1_g, ln1_b=ln1_b, ln2_g=ln2_g, ln2_b=ln2_b)

    cfg_p = dict(PROMPT_CFG, nb=bp, ls=seq, a_valid=PROMPT_CFG["a_lt"], b_valid=PROMPT_CFG["b_lt"])
    mem_rows = mem_prompt.reshape(bp * MEM_LEN, D_MODEL)
    sc0 = jnp.zeros((bp, SC_CONV_W - 1, SC_DIM), F32)
    cv0 = jnp.zeros((bp, SSD_CONV_W - 1, SSD_CONV_DIM), F32)
    h0 = jnp.zeros((bp, SSD_HEADS, SSD_HEADDIM, SSD_STATE), F32)
    yp = x_prompt.reshape(bp * seq, D_MODEL)
    mk_p, mv_p, sc_p, cv_p, h_p = [], [], [], [], []
    for l in range(DEPTH):
        kv = _matmul(mem_rows, wp["w_mem_kv"][l], 1024, 1024)
        mk = kv[:, :D_MODEL].reshape(bp, MEM_LEN, D_MODEL)
        mv = kv[:, D_MODEL:].reshape(bp, MEM_LEN, D_MODEL)
        yp, sc_n, cv_n, h_n = _decoder_layer(l, yp, mk, mv, sc0, cv0, h0, wp, p, cfg_p)
        mk_p.append(mk.reshape(bp, MEM_LEN, MEM_HEADS, MEM_HEAD_DIM))
        mv_p.append(mv.reshape(bp, MEM_LEN, MEM_HEADS, MEM_HEAD_DIM))
        sc_p.append(sc_n)
        cv_p.append(cv_n)
        h_p.append(h_n)

    ls = SAMPLE_PAD
    cfg_s = dict(tm_proj=bs * ls, nb=bs, ls=ls, a_nseq=16, a_lt=ls, a_valid=dec_seq, b_nseq=4, b_lt=ls, b_q=ls,
                 b_valid=dec_seq, m_nseq=4, m_lt=ls, tm_merge=512, moe_tc=bs * ls)
    ys = jnp.pad(x_sample, ((0, 0), (0, ls - dec_seq), (0, 0))).reshape(bs * ls, D_MODEL)
    sc_s, cv_s, h_s = [], [], []
    for l in range(DEPTH):
        ys, sc_n, cv_n, h_n = _decoder_layer(
            l, ys, cache_mem_k[l].reshape(bs, MEM_LEN, D_MODEL), cache_mem_v[l].reshape(bs, MEM_LEN, D_MODEL),
            state_sconv[l], state_ssd_conv[l], state_ssd[l], wp, p, cfg_s)
        sc_s.append(sc_n)
        cv_s.append(cv_n)
        h_s.append(h_n)

    y_prompt = yp.reshape(bp, seq, D_MODEL)
    y_sample = ys.reshape(bs, ls, D_MODEL)[:, :dec_seq]
    return (y_prompt, y_sample, jnp.stack(mk_p), jnp.stack(mv_p), jnp.stack(sc_p), jnp.stack(cv_p), jnp.stack(h_p),
            jnp.stack(sc_s), jnp.stack(cv_s), jnp.stack(h_s))
```

```python
import functools

import jax
import jax.numpy as jnp
from jax import lax
from jax.experimental import pallas as pl
from jax.experimental.pallas import tpu as pltpu

F32 = jnp.float32
BF16 = jnp.bfloat16

D_MODEL = 1024
DEPTH = 4
SC_DIM = D_MODEL
SC_CONV_W = 3
SSD_INNER = 2 * D_MODEL
SSD_HEADDIM = 64
SSD_HEADS = SSD_INNER // SSD_HEADDIM
SSD_GROUPS = 4
SSD_STATE = 128
SSD_CONV_W = 4
SSD_BC = SSD_GROUPS * SSD_STATE
SSD_CONV_DIM = SSD_INNER + 2 * SSD_BC
SSD_CHUNK = 128
HEADS_PER_GROUP = SSD_HEADS // SSD_GROUPS
GROUP_INNER = SSD_INNER // SSD_GROUPS
MEM_LEN = 256
MEM_HEADS = 4
MEM_HEAD_DIM = D_MODEL // MEM_HEADS
N_EXP_GROUPS = 4
EXP_PER_GROUP = 8
N_EXPERTS = N_EXP_GROUPS * EXP_PER_GROUP
EXP_TOPK = 2
EXP_FF = 512
MOE_BLOCK = 128
ALPHA = (2 * DEPTH) ** 0.25
LN_EPS = 1e-5
RMS_EPS = 1e-6

LANES = 128
SUBLANES = 8
VMEM_LIMIT = 56 * 1024 * 1024

COL_XBC = 0
COL_BCX = SSD_CONV_DIM
COL_Z = COL_BCX + 3 * SC_DIM
COL_Q = COL_Z + SSD_INNER
COL_G = COL_Q + D_MODEL
PROJ_COLS = COL_G + 3 * D_MODEL


def _dot(a, b):
    return jnp.dot(a, b, preferred_element_type=F32)


def _dot_exact(a, b):
    return jnp.dot(a, b, preferred_element_type=F32, precision=lax.Precision.HIGHEST)


def _split_dot(a, b, terms, dims=(((1,), (0,)), ((), ()))):
    acc = None
    rest = a
    for t in range(terms):
        piece = rest.astype(BF16)
        part = lax.dot_general(piece, b, dims, preferred_element_type=F32)
        acc = part if acc is None else acc + part
        if t + 1 < terms:
            rest = rest - piece.astype(F32)
    return acc


def _sigmoid(x):
    return 1.0 / (1.0 + jnp.exp(-x))


def _silu(x):
    return x * _sigmoid(x)


def _layer_norm(x, g, b):
    mu = jnp.mean(x, axis=-1, keepdims=True)
    xc = x - mu
    var = jnp.mean(xc * xc, axis=-1, keepdims=True)
    return xc * lax.rsqrt(var + LN_EPS) * g + b


def _params(*sem):
    return pltpu.CompilerParams(dimension_semantics=sem, vmem_limit_bytes=VMEM_LIMIT)


def _skip_alias_refs(kernel_fn, n_in, n_alias):
    def wrapped(*refs):
        return kernel_fn(*refs[:n_in], *refs[n_in + n_alias:])
    return wrapped


def _alias_args(prevs, n_in, first_out):
    prevs = [p for p in prevs if p is not None]
    specs = [pl.BlockSpec(memory_space=pl.ANY)] * len(prevs)
    aliases = {n_in + k: first_out + k for k in range(len(prevs))}
    return prevs, specs, aliases


def _mm_kernel(x_ref, w_ref, o_ref):
    o_ref[...] = _dot(x_ref[...].astype(BF16), w_ref[...])


def _matmul(x, w, layer, tm, tn, col0=0, ncols=None, stacked_prev=None, stacked=False):
    m, k = x.shape
    ncols = w.shape[2] if ncols is None else ncols
    tm = min(tm, m)
    tn = min(tn, ncols)
    cblk0 = col0 // tn
    in_specs = [pl.BlockSpec((tm, k), lambda i, j: (i, 0)),
                pl.BlockSpec((None, k, tn), lambda i, j: (layer, 0, cblk0 + j))]
    args = [x, w]
    kern = _mm_kernel
    aliases = {}
    if stacked:
        out_spec = pl.BlockSpec((None, tm, tn), lambda i, j: (layer, i, j))
        out_shape = jax.ShapeDtypeStruct((DEPTH, m, ncols), F32)
        prevs, pspecs, aliases = _alias_args([stacked_prev], 2, 0)
        kern = _skip_alias_refs(_mm_kernel, 2, len(prevs))
        in_specs += pspecs
        args += prevs
    else:
        out_spec = pl.BlockSpec((tm, tn), lambda i, j: (i, j))
        out_shape = jax.ShapeDtypeStruct((m, ncols), F32)
    return pl.pallas_call(
        kern,
        grid=(m // tm, ncols // tn),
        in_specs=in_specs,
        out_specs=out_spec,
        out_shape=out_shape,
        input_output_aliases=aliases,
        compiler_params=_params("parallel", "arbitrary"),
        name="matmul",
    )(*args)


def _branch_a_kernel(bcx_ref, g_ref, prev_ref, wc_ref, wout_ref, o_ref, st_ref, ext_ref, *, nseq, lt, valid):
    c_dim = SC_DIM
    halo = SC_CONV_W - 1
    base = SUBLANES
    rows = nseq * lt
    bcx = bcx_ref[...]
    scb = bcx[:, :c_dim]
    u = (bcx[:, c_dim:2 * c_dim] * bcx[:, 2 * c_dim:]).reshape(nseq, lt, c_dim)

    @pl.when(pl.program_id(1) == 0)
    def _():
        ext_ref[:, base - halo:base, :] = prev_ref[...]

    ext_ref[:, base:base + lt, :] = u
    w = wc_ref[...]
    v = ext_ref[:, base - 2:base - 2 + lt, :] * w[0:1, :]
    v = v + ext_ref[:, base - 1:base - 1 + lt, :] * w[1:2, :]
    v = v + u * w[2:3, :]
    st_ref[...] = ext_ref[:, base + valid - halo:base + valid, :]
    ext_ref[:, base - halo:base, :] = ext_ref[:, base + lt - halo:base + lt, :]
    ya = _dot((scb * v.reshape(rows, c_dim)).astype(BF16), wout_ref[...])
    o_ref[...] = _sigmoid(g_ref[...]) * ya


def _branch_a(proj, prev, prev_layer, st_prev, w_conv, w_out, *, layer, nb, ls, nseq, lt, valid):
    nj = ls // lt
    rows = nseq * lt
    n_in = 5
    prevs, pspecs, aliases = _alias_args([st_prev], n_in, 1)
    kern = _skip_alias_refs(functools.partial(_branch_a_kernel, nseq=nseq, lt=lt, valid=valid), n_in, len(prevs))
    return pl.pallas_call(
        kern,
        grid=(nb // nseq, nj),
        in_specs=[pl.BlockSpec((rows, 3 * SC_DIM), lambda i, j: (i * nj + j, COL_BCX // (3 * SC_DIM))),
                  pl.BlockSpec((rows, D_MODEL), lambda i, j: (i * nj + j, COL_G // D_MODEL)),
                  pl.BlockSpec((None, nseq, SC_CONV_W - 1, SC_DIM), lambda i, j: (prev_layer, i, 0, 0)),
                  pl.BlockSpec((None, SC_CONV_W, SC_DIM), lambda i, j: (layer, 0, 0)),
                  pl.BlockSpec((None, SC_DIM, D_MODEL), lambda i, j: (layer, 0, 0))] + pspecs,
        out_specs=[pl.BlockSpec((rows, D_MODEL), lambda i, j: (i * nj + j, 0)),
                   pl.BlockSpec((None, nseq, SC_CONV_W - 1, SC_DIM), lambda i, j: (layer, i, 0, 0))],
        out_shape=[jax.ShapeDtypeStruct((nb * ls, D_MODEL), F32),
                   jax.ShapeDtypeStruct((DEPTH, nb, SC_CONV_W - 1, SC_DIM), F32)],
        scratch_shapes=[pltpu.VMEM((nseq, SUBLANES + lt, SC_DIM), F32)],
        input_output_aliases=aliases,
        compiler_params=_params("arbitrary", "arbitrary"),
        name="branch_a",
    )(proj, proj, prev, w_conv, w_out, *prevs)


def _branch_m_kernel(q_ref, g_ref, k_ref, v_ref, wout_ref, o_ref, *, nseq, lt):
    rows = nseq * lt
    scale = MEM_HEAD_DIM ** -0.5
    qb = q_ref[...].astype(BF16).reshape(nseq, lt, D_MODEL)
    outs = []
    for h in range(MEM_HEADS):
        sl = slice(h * MEM_HEAD_DIM, (h + 1) * MEM_HEAD_DIM)
        kh = k_ref[:, :, sl].astype(BF16)
        vh = v_ref[:, :, sl].astype(BF16)
        s = jnp.einsum("sld,smd->slm", qb[:, :, sl], kh, preferred_element_type=F32) * scale
        s = s - jnp.max(s, axis=-1, keepdims=True)
        p = jnp.exp(s)
        p = p / jnp.sum(p, axis=-1, keepdims=True)
        outs.append(jnp.einsum("slm,smd->sld", p.astype(BF16), vh, preferred_element_type=F32))
    o = jnp.concatenate(outs, axis=-1).reshape(rows, D_MODEL)
    o_ref[...] = _sigmoid(g_ref[...]) * _dot(o.astype(BF16), wout_ref[...])


def _branch_m(proj, mem_k, mem_v, w_out, *, layer, nb, ls, nseq, lt):
    nj = ls // lt
    rows = nseq * lt
    kern = functools.partial(_branch_m_kernel, nseq=nseq, lt=lt)
    kv_spec = pl.BlockSpec((None, nseq, MEM_LEN, D_MODEL), lambda i, j: (layer, i, 0, 0))
    return pl.pallas_call(
        kern,
        grid=(nb // nseq, nj),
        in_specs=[pl.BlockSpec((rows, D_MODEL), lambda i, j: (i * nj + j, COL_Q // D_MODEL)),
                  pl.BlockSpec((rows, D_MODEL), lambda i, j: (i * nj + j, COL_G // D_MODEL + 2)),
                  kv_spec, kv_spec,
                  pl.BlockSpec((None, D_MODEL, D_MODEL), lambda i, j: (layer, 0, 0))],
        out_specs=pl.BlockSpec((rows, D_MODEL), lambda i, j: (i * nj + j, 0)),
        out_shape=jax.ShapeDtypeStruct((nb * ls, D_MODEL), F32),
        compiler_params=_params("parallel", "arbitrary"),
        name="branch_m",
    )(proj, proj, mem_k, mem_v, w_out)


def _softplus(x):
    return jnp.maximum(x, 0.0) + jnp.log1p(jnp.exp(-jnp.abs(x)))


def _ssd_chunk(s, c, xc_ref, z_ref, dt_ref, h_ref, yn_ref, a_row, d_row, nw_row, expand, *, lt, q):
    r0 = pl.multiple_of(c * q, q)
    row0 = pl.multiple_of(s * lt + c * q, q)
    xs = xc_ref[s, pl.ds(r0, q), 0:SSD_INNER]
    bm = xc_ref[s, pl.ds(r0, q), SSD_INNER:SSD_INNER + SSD_BC].astype(BF16)
    cm = xc_ref[s, pl.ds(r0, q), SSD_INNER + SSD_BC:SSD_CONV_DIM].astype(BF16)
    dt = dt_ref[pl.ds(row0, q), :]
    da = dt * a_row
    ti = lax.broadcasted_iota(jnp.int32, (q, q), 0)
    si = lax.broadcasted_iota(jnp.int32, (q, q), 1)
    causal = si <= ti
    tri = jnp.where(causal, 1.0, 0.0).astype(BF16)
    acs = _tri_cumsum(tri, da)
    acs_t = _split_dot(da, tri, 3, dims=(((0,), (1,)), ((), ())))
    acs_last = acs[q - 1:q, :]
    stack = jnp.concatenate([dt, jnp.exp(acs), jnp.exp(acs_last - acs)], axis=0)
    stack_e = _split_dot(stack, expand, 2)
    dt_e, eacs_e, dte_e = stack_e[0:q], stack_e[q:2 * q], stack_e[2 * q:3 * q]
    xdt = xs * dt_e
    xdt_b = xdt.astype(BF16)
    xdte_b = (xdt * dte_e).astype(BF16)
    dec_full = jnp.broadcast_to(jnp.exp(acs_t[:, q - 1:q]), (LANES, SSD_STATE))
    lane = lax.broadcasted_iota(jnp.int32, (q, LANES), 1)
    lo_mask = lane < SSD_HEADDIM
    y_groups = []
    for g in range(SSD_GROUPS):
        gs = slice(g * SSD_STATE, (g + 1) * SSD_STATE)
        gi = slice(g * GROUP_INNER, (g + 1) * GROUP_INNER)
        cb = lax.dot_general(cm[:, gs], bm[:, gs], (((1,), (1,)), ((), ())), preferred_element_type=F32)
        h0 = g * HEADS_PER_GROUP
        hg = h_ref[s, h0:h0 + HEADS_PER_GROUP].reshape(GROUP_INNER, SSD_STATE)
        y_int = lax.dot_general(cm[:, gs], hg.astype(BF16), (((1,), (1,)), ((), ())),
                                preferred_element_type=F32) * eacs_e[:, gi]
        upd = lax.dot_general(xdte_b[:, gi], bm[:, gs], (((0,), (0,)), ((), ())), preferred_element_type=F32)
        y_pairs = []
        for pr in range(HEADS_PER_GROUP // 2):
            col = g * GROUP_INNER + pr * LANES
            xpair = xdt_b[:, col:col + LANES]
            acc = None
            for hh in range(2):
                h = h0 + 2 * pr + hh
                seg = acs[:, h:h + 1] - acs_t[h:h + 1, :]
                lm = jnp.exp(jnp.where(causal, seg, -jnp.inf))
                m = (cb * lm).astype(BF16)
                keep = lo_mask if hh == 0 else jnp.logical_not(lo_mask)
                part = _dot(m, jnp.where(keep, xpair, jnp.zeros_like(xpair)))
                acc = part if acc is None else acc + part
            y_pairs.append(acc)
        y_groups.append(jnp.concatenate(y_pairs, axis=-1) + y_int)
        for hh in range(HEADS_PER_GROUP):
            h = h0 + hh
            h_ref[s, h] = h_ref[s, h] * dec_full[h:h + 1, :] + upd[hh * SSD_HEADDIM:(hh + 1) * SSD_HEADDIM, :]
    y = jnp.concatenate(y_groups, axis=-1) + d_row * xs
    yf = y * _silu(z_ref[pl.ds(row0, q), :])
    normed = []
    for g in range(SSD_GROUPS):
        yg = yf[:, g * GROUP_INNER:(g + 1) * GROUP_INNER]
        normed.append(yg * lax.rsqrt(jnp.mean(yg * yg, axis=-1, keepdims=True) + RMS_EPS))
    yn_ref[pl.ds(row0, q), :] = jnp.concatenate(normed, axis=-1) * nw_row


def _tri_cumsum(tri, da):
    acc = None
    rest = da
    for t in range(3):
        piece = rest.astype(BF16)
        part = _dot(tri, piece)
        acc = part if acc is None else acc + part
        if t < 2:
            rest = rest - piece.astype(F32)
    return acc


def _branch_b_kernel(xbc_ref, z_ref, dtr_ref, g_ref, cprev_ref, h0_ref, wc_ref, bc_ref, dtb_ref, alog_ref,
                     d_ref, nw_ref, exp_ref, wout_ref, o_ref, cst_ref, h_ref, ext_ref, xc_ref, dt_ref, yn_ref,
                     *, nseq, lt, q, valid):
    halo = SSD_CONV_W - 1
    base = SUBLANES
    rows = nseq * lt

    @pl.when(pl.program_id(1) == 0)
    def _():
        ext_ref[:, base - halo:base, :] = cprev_ref[...]
        h_ref[...] = h0_ref[...]

    ext_ref[:, base:base + lt, :] = xbc_ref[...].reshape(nseq, lt, SSD_CONV_DIM)
    w = wc_ref[...]
    conv = ext_ref[:, base - 3:base - 3 + lt, :] * w[0:1, :]
    for k in range(1, SSD_CONV_W):
        conv = conv + ext_ref[:, base - 3 + k:base - 3 + k + lt, :] * w[k:k + 1, :]
    cst_ref[...] = ext_ref[:, base + valid - halo:base + valid, :]
    ext_ref[:, base - halo:base, :] = ext_ref[:, base + lt - halo:base + lt, :]
    xc_ref[...] = _silu(conv + bc_ref[...])
    dt = _softplus(dtr_ref[...] + dtb_ref[...])
    if valid < lt:
        t_in_seq = lax.broadcasted_iota(jnp.int32, (nseq, lt, LANES), 1).reshape(rows, LANES)
        dt = jnp.where(t_in_seq < valid, dt, 0.0)
    dt_ref[...] = dt
    a_row = -jnp.exp(alog_ref[...])
    d_row = d_ref[...]
    nw_row = nw_ref[...]
    expand = exp_ref[...]
    nchunk = lt // q

    def body(n, carry):
        _ssd_chunk(n // nchunk, n % nchunk, xc_ref, z_ref, dt_ref, h_ref, yn_ref, a_row, d_row, nw_row, expand,
                   lt=lt, q=q)
        return carry

    lax.fori_loop(0, nseq * nchunk, body, 0)
    o_ref[...] = _sigmoid(g_ref[...]) * _dot(yn_ref[...].astype(BF16), wout_ref[...])


def _branch_b(proj, dtr, cprev, h0, prev_layer, cst_prev, hst_prev, w_conv, b_conv, dt_bias, a_log, d_exp, norm_w,
              expand, w_out, *, layer, nb, ls, nseq, lt, q, valid):
    nj = ls // lt
    rows = nseq * lt
    n_in = 14
    prevs, pspecs, aliases = _alias_args([cst_prev, hst_prev], n_in, 1)
    kern = _skip_alias_refs(functools.partial(_branch_b_kernel, nseq=nseq, lt=lt, q=q, valid=valid),
                            n_in, len(prevs))
    row_blk = lambda i, j: (i * nj + j, 0)
    lay3 = lambda i, j: (layer, 0, 0)
    return pl.pallas_call(
        kern,
        grid=(nb // nseq, nj),
        in_specs=[pl.BlockSpec((rows, SSD_CONV_DIM), lambda i, j: (i * nj + j, COL_XBC // SSD_CONV_DIM)),
                  pl.BlockSpec((rows, SSD_INNER), lambda i, j: (i * nj + j, COL_Z // SSD_INNER)),
                  pl.BlockSpec((rows, LANES), row_blk),
                  pl.BlockSpec((rows, D_MODEL), lambda i, j: (i * nj + j, COL_G // D_MODEL + 1)),
                  pl.BlockSpec((None, nseq, SSD_CONV_W - 1, SSD_CONV_DIM), lambda i, j: (prev_layer, i, 0, 0)),
                  pl.BlockSpec((None, nseq, SSD_HEADS, SSD_HEADDIM, SSD_STATE),
                               lambda i, j: (prev_layer, i, 0, 0, 0)),
                  pl.BlockSpec((None, SSD_CONV_W, SSD_CONV_DIM), lay3),
                  pl.BlockSpec((None, 1, SSD_CONV_DIM), lay3),
                  pl.BlockSpec((None, 1, LANES), lay3),
                  pl.BlockSpec((None, 1, LANES), lay3),
                  pl.BlockSpec((None, 1, SSD_INNER), lay3),
                  pl.BlockSpec((None, 1, SSD_INNER), lay3),
                  pl.BlockSpec((LANES, SSD_INNER), lambda i, j: (0, 0)),
                  pl.BlockSpec((None, SSD_INNER, D_MODEL), lay3)] + pspecs,
        out_specs=[pl.BlockSpec((rows, D_MODEL), row_blk),
                   pl.BlockSpec((None, nseq, SSD_CONV_W - 1, SSD_CONV_DIM), lambda i, j: (layer, i, 0, 0)),
                   pl.BlockSpec((None, nseq, SSD_HEADS, SSD_HEADDIM, SSD_STATE),
                                lambda i, j: (layer, i, 0, 0, 0))],
        out_shape=[jax.ShapeDtypeStruct((nb * ls, D_MODEL), F32),
                   jax.ShapeDtypeStruct((DEPTH, nb, SSD_CONV_W - 1, SSD_CONV_DIM), F32),
                   jax.ShapeDtypeStruct((DEPTH, nb, SSD_HEADS, SSD_HEADDIM, SSD_STATE), F32)],
        scratch_shapes=[pltpu.VMEM((nseq, SUBLANES + lt, SSD_CONV_DIM), F32),
                        pltpu.VMEM((nseq, lt, SSD_CONV_DIM), F32),
                        pltpu.VMEM((rows, LANES), F32),
                        pltpu.VMEM((rows, SSD_INNER), F32)],
        input_output_aliases=aliases,
        compiler_params=_params("arbitrary", "arbitrary"),
        name="branch_b",
    )(proj, proj, dtr, proj, cprev, h0, w_conv, b_conv, dt_bias, a_log, d_exp, norm_w, expand, w_out, *prevs)


def _merge_kernel(ya_ref, yb_ref, ym_ref, x_ref, wo_ref, g_ref, b_ref, wr_ref, br_ref, x1_ref, route_ref):
    merged = (ya_ref[...] + yb_ref[...]) + ym_ref[...]
    mix = _dot(merged.astype(BF16), wo_ref[...])
    x1 = _layer_norm(ALPHA * x_ref[...] + mix, g_ref[...], b_ref[...])
    x1_ref[...] = x1
    logits = _dot_exact(x1, wr_ref[...]) + br_ref[...]
    lane = lax.broadcasted_iota(jnp.int32, logits.shape, 1).astype(F32)
    neg = -jnp.inf
    big = float(LANES)
    is_grp = lane < N_EXP_GROUPS
    gl = jnp.where(is_grp, logits, neg)
    gmax = jnp.max(gl, axis=-1, keepdims=True)
    gsum = jnp.sum(jnp.exp(gl - gmax), axis=-1, keepdims=True)
    g_idx = jnp.min(jnp.where(jnp.logical_and(is_grp, gl == gmax), lane, big), axis=-1, keepdims=True)
    g_w = 1.0 / gsum
    e_lo = N_EXP_GROUPS + g_idx * EXP_PER_GROUP
    in_grp = jnp.logical_and(lane >= e_lo, lane < e_lo + EXP_PER_GROUP)
    el = jnp.where(in_grp, logits, neg)
    emax = jnp.max(el, axis=-1, keepdims=True)
    ee = jnp.exp(el - emax)
    esum = jnp.sum(ee, axis=-1, keepdims=True)
    i0 = jnp.min(jnp.where(jnp.logical_and(in_grp, el == emax), lane, big), axis=-1, keepdims=True)
    el2 = jnp.where(lane == i0, neg, el)
    emax2 = jnp.max(el2, axis=-1, keepdims=True)
    i1 = jnp.min(jnp.where(jnp.logical_and(in_grp, el2 == emax2), lane, big), axis=-1, keepdims=True)
    w0 = 1.0 / esum
    w1 = jnp.exp(emax2 - emax) / esum
    wsum = w0 + w1
    gate0 = g_w * w0 / wsum
    gate1 = g_w * w1 / wsum
    id0 = i0 - N_EXP_GROUPS
    id1 = i1 - N_EXP_GROUPS
    route = jnp.where(lane == 0, id0, jnp.where(lane == 1, id1, jnp.where(lane == 2, gate0,
                      jnp.where(lane == 3, gate1, 0.0))))
    route_ref[...] = route


def _merge(ya, yb, ym, x, w_o, ln_g, ln_b, w_r, b_r, *, layer, tm):
    t = x.shape[0]
    row = lambda i: (i, 0)
    lay = lambda i: (layer, 0, 0)
    return pl.pallas_call(
        _merge_kernel,
        grid=(t // tm,),
        in_specs=[pl.BlockSpec((tm, D_MODEL), row)] * 4 + [
            pl.BlockSpec((None, D_MODEL, D_MODEL), lay),
            pl.BlockSpec((None, 1, D_MODEL), lay), pl.BlockSpec((None, 1, D_MODEL), lay),
            pl.BlockSpec((None, D_MODEL, LANES), lay), pl.BlockSpec((None, 1, LANES), lay)],
        out_specs=[pl.BlockSpec((tm, D_MODEL), row), pl.BlockSpec((tm, LANES), row)],
        out_shape=[jax.ShapeDtypeStruct((t, D_MODEL), F32), jax.ShapeDtypeStruct((t, LANES), F32)],
        compiler_params=_params("parallel"),
        name="merge_ln_router",
    )(ya, yb, ym, x, w_o, ln_g, ln_b, w_r, b_r)


def _moe_kernel(bexp_ref, nused_ref, starts_ref, pstarts_ref, counts_ref, tok_ref, row_ref, gate_ref, x_ref,
                win_ref, wout_ref, g_ref, b_ref, o_ref, xg_ref, eo_ref, y2_ref, *, nblk, tc):
    c = pl.program_id(0)
    i = pl.program_id(1)
    unroll = 8

    @pl.when(i < nused_ref[c])
    def _():
        tbl = c * N_EXPERTS + bexp_ref[c * nblk + i]
        start = starts_ref[tbl]
        last = counts_ref[tbl] - 1
        off0 = i * MOE_BLOCK - pstarts_ref[tbl]

        def gather(r, carry):
            tok = tok_ref[0, 0, start + jnp.minimum(off0 + r, last)]
            xg_ref[pl.ds(r, 1), :] = x_ref[pl.ds(tok, 1), :]
            return carry

        lax.fori_loop(0, MOE_BLOCK, gather, 0, unroll=unroll)
        gu = _dot(xg_ref[...].astype(BF16), win_ref[...])
        hid = _silu(gu[:, :EXP_FF]) * gu[:, EXP_FF:]
        eo_ref[...] = _dot(hid.astype(BF16), wout_ref[...])

        def scatter(r, carry):
            off = off0 + r
            pos = start + jnp.minimum(off, last)
            row = jnp.where(off <= last, row_ref[0, 0, pos], EXP_TOPK * tc)
            y2_ref[pl.ds(row, 1), :] = gate_ref[0, 0, pos] * eo_ref[pl.ds(r, 1), :]
            return carry

        lax.fori_loop(0, MOE_BLOCK, scatter, 0, unroll=unroll)

    @pl.when(i == nblk - 1)
    def _():
        ffn = y2_ref[0:tc, :] + y2_ref[tc:2 * tc, :]
        o_ref[...] = _layer_norm(ALPHA * x_ref[...] + ffn, g_ref[...], b_ref[...])


def _moe(x1, tables, slot_arrays, w_in, w_out, ln_g, ln_b, *, layer, tc, nblk):
    t = x1.shape[0]
    nchunks = t // tc
    s = tc * EXP_TOPK
    kern = functools.partial(_moe_kernel, nblk=nblk, tc=tc)
    smem_row = lambda c, i, *_: (c, 0, 0)
    lay = lambda c, i, *_: (layer, 0, 0)
    expert = lambda c, i, be, *_: (layer, be[c * nblk + i], 0, 0)
    grid_spec = pltpu.PrefetchScalarGridSpec(
        num_scalar_prefetch=5,
        grid=(nchunks, nblk),
        in_specs=[pl.BlockSpec((1, 1, s), smem_row, memory_space=pltpu.SMEM)] * 3 + [
                  pl.BlockSpec((tc, D_MODEL), lambda c, i, *_: (c, 0), pipeline_mode=pl.Buffered(1)),
                  pl.BlockSpec((None, None, D_MODEL, 2 * EXP_FF), expert),
                  pl.BlockSpec((None, None, EXP_FF, D_MODEL), expert),
                  pl.BlockSpec((None, 1, D_MODEL), lay),
                  pl.BlockSpec((None, 1, D_MODEL), lay)],
        out_specs=pl.BlockSpec((tc, D_MODEL), lambda c, i, *_: (c, 0)),
        scratch_shapes=[pltpu.VMEM((MOE_BLOCK, D_MODEL), F32), pltpu.VMEM((MOE_BLOCK, D_MODEL), F32),
                        pltpu.VMEM((EXP_TOPK * tc + SUBLANES, D_MODEL), F32)],
    )
    return pl.pallas_call(
        kern,
        grid_spec=grid_spec,
        out_shape=jax.ShapeDtypeStruct((t, D_MODEL), F32),
        compiler_params=_params("arbitrary", "arbitrary"),
        name="moe_experts",
    )(*tables, *[a[:, None, :] for a in slot_arrays], x1, w_in, w_out, ln_g, ln_b)


def _dispatch(route, *, tc, nblk):
    t = route.shape[0]
    nchunks = t // tc
    s = tc * EXP_TOPK
    ids = route[:, 0:EXP_TOPK].astype(jnp.int32).reshape(nchunks, s)
    gates = route[:, EXP_TOPK:2 * EXP_TOPK].reshape(nchunks, s)
    slot = jnp.broadcast_to(jnp.arange(s, dtype=jnp.int32)[None, :], (nchunks, s))
    _, order, gate_sorted = lax.sort((ids, slot, gates), dimension=1, is_stable=True, num_keys=1)
    tok_sorted = order // EXP_TOPK
    row_sorted = (order % EXP_TOPK) * tc + tok_sorted
    counts =jnp.sum((ids[:, :, None] == jnp.arange(N_EXPERTS, dtype=jnp.int32)).astype(jnp.int32), axis=1)
    starts = jnp.cumsum(counts, axis=-1) - counts
    padded = (counts + MOE_BLOCK - 1) // MOE_BLOCK * MOE_BLOCK
    pends = jnp.cumsum(padded, axis=-1)
    pstarts = pends - padded
    blk_start = jnp.arange(nblk, dtype=jnp.int32) * MOE_BLOCK
    bexp = jnp.sum((pends[:, None, :] <= blk_start[None, :, None]).astype(jnp.int32), axis=-1)
    bexp = jnp.minimum(bexp, N_EXPERTS - 1).reshape(-1)
    nused = pends[:, -1] // MOE_BLOCK
    tables = (bexp, nused, starts.reshape(-1), pstarts.reshape(-1), counts.reshape(-1))
    return tables, (tok_sorted, row_sorted, gate_sorted)


def _prep_weights(w_in, w_sconv_out, w_ssd_out, w_mem_k, w_mem_v, w_mem_out, w_o, w_exp_in, w_exp_out,
                  w_router_grp, b_router_grp, w_router_exp, b_router_exp, ssd_dt_bias, ssd_a_log, ssd_d):
    o_b, o_z, o_xbc, o_dt, o_q, o_g = 0, 3072, 5120, 8192, 8224, 9248
    w_main = jnp.concatenate([w_in[:, :, o_xbc:o_dt], w_in[:, :, o_b:o_z], w_in[:, :, o_z:o_xbc],
                              w_in[:, :, o_q:o_g], w_in[:, :, o_g:]], axis=-1).astype(BF16)
    pad_h = LANES - SSD_HEADS
    w_dt = jnp.pad(w_in[:, :, o_dt:o_q], ((0, 0), (0, 0), (0, pad_h))).astype(BF16)
    pad_r = LANES - N_EXP_GROUPS - N_EXPERTS
    w_r = jnp.pad(jnp.concatenate([w_router_grp, w_router_exp], axis=-1), ((0, 0), (0, 0), (0, pad_r)))
    b_r = jnp.pad(jnp.concatenate([b_router_grp, b_router_exp], axis=-1), ((0, 0), (0, pad_r)))[:, None, :]
    head_of_col = jnp.arange(SSD_INNER, dtype=jnp.int32) // SSD_HEADDIM
    expand = (jnp.arange(LANES, dtype=jnp.int32)[:, None] == head_of_col[None, :]).astype(BF16)
    return dict(
        w_main=w_main, w_dt=w_dt, w_r=w_r, b_r=b_r, expand=expand,
        w_sconv_out=w_sconv_out.astype(BF16), w_ssd_out=w_ssd_out.astype(BF16),
        w_mem_kv=jnp.concatenate([w_mem_k, w_mem_v], axis=-1).astype(BF16),
        w_mem_out=w_mem_out.astype(BF16), w_o=w_o.astype(BF16),
        w_exp_in=w_exp_in.astype(BF16), w_exp_out=w_exp_out.astype(BF16),
        dt_bias=jnp.pad(ssd_dt_bias, ((0, 0), (0, pad_h)))[:, None, :],
        a_log=jnp.pad(ssd_a_log, ((0, 0), (0, pad_h)))[:, None, :],
        d_exp=jnp.repeat(ssd_d, SSD_HEADDIM, axis=-1)[:, None, :],
    )


def _moe_blocks(tc):
    return (tc * EXP_TOPK + N_EXPERTS * (MOE_BLOCK - 1) + MOE_BLOCK - 1) // MOE_BLOCK


def _decoder_layer(l, x, mem_k, mem_v, states_in, prev_layer, states_out, wp, p, cfg):
    nb, ls = cfg["nb"], cfg["ls"]
    sc_in, cv_in, h_in = states_in
    sc_out, cv_out, h_out = states_out
    proj = _matmul(x, wp["w_main"], l, cfg["tm_proj"], 1024)
    dtr = _matmul(x, wp["w_dt"], l, cfg["tm_proj"], LANES)
    ya, sc_new = _branch_a(proj, sc_in, prev_layer, sc_out, p["w_sconv"], wp["w_sconv_out"],
                           layer=l, nb=nb, ls=ls, nseq=cfg["a_nseq"], lt=cfg["a_lt"], valid=cfg["a_valid"])
    yb, cv_new, h_new = _branch_b(proj, dtr, cv_in, h_in, prev_layer, cv_out, h_out, p["w_ssd_conv"],
                                  p["b_ssd_conv"], wp["dt_bias"], wp["a_log"], wp["d_exp"], p["ssd_norm_w"],
                                  wp["expand"], wp["w_ssd_out"],
                                  layer=l, nb=nb, ls=ls, nseq=cfg["b_nseq"], lt=cfg["b_lt"], q=cfg["b_q"],
                                  valid=cfg["b_valid"])
    ym = _branch_m(proj, mem_k, mem_v, wp["w_mem_out"], layer=l, nb=nb, ls=ls, nseq=cfg["m_nseq"],
                   lt=cfg["m_lt"])
    x1, route = _merge(ya, yb, ym, x, wp["w_o"], p["ln1_g"], p["ln1_b"], wp["w_r"], wp["b_r"],
                       layer=l, tm=cfg["tm_merge"])
    tc = cfg["moe_tc"]
    nblk = _moe_blocks(tc)
    tables, slot_arrays = _dispatch(route, tc=tc, nblk=nblk)
    x2 = _moe(x1, tables, slot_arrays, wp["w_exp_in"], wp["w_exp_out"], p["ln2_g"], p["ln2_b"],
              layer=l, tc=tc, nblk=nblk)
    return x2, (sc_new, cv_new, h_new)


PROMPT_CFG = dict(tm_proj=1024, a_nseq=1, a_lt=512, b_nseq=1, b_lt=256, b_q=SSD_CHUNK, m_nseq=1, m_lt=512,
                  tm_merge=512, moe_tc=2048)
SAMPLE_PAD = SUBLANES


def kernel(x_prompt, x_sample, mem_prompt, cache_mem_k, cache_mem_v, state_sconv, state_ssd_conv, state_ssd, w_in, w_sconv, w_sconv_out, w_ssd_conv, b_ssd_conv, ssd_dt_bias, ssd_a_log, ssd_d, ssd_norm_w, w_ssd_out, w_mem_k, w_mem_v, w_mem_out, w_o, ln1_g, ln1_b, w_router_grp, b_router_grp, w_router_exp, b_router_exp, w_exp_in, w_exp_out, ln2_g, ln2_b):
    bp, seq, _ = x_prompt.shape
    bs, dec_seq, _ = x_sample.shape
    wp = _prep_weights(w_in, w_sconv_out, w_ssd_out, w_mem_k, w_mem_v, w_mem_out, w_o, w_exp_in, w_exp_out,
                       w_router_grp, b_router_grp, w_router_exp, b_router_exp, ssd_dt_bias, ssd_a_log, ssd_d)
    p = dict(w_sconv=w_sconv, w_ssd_conv=w_ssd_conv, b_ssd_conv=b_ssd_conv[:, None, :],
             ssd_norm_w=ssd_norm_w[:, None, :], ln1_g=ln1_g[:, None, :], ln1_b=ln1_b[:, None, :],
             ln2_g=ln2_g[:, None, :], ln2_b=ln2_b[:, None, :])

    cfg_p = dict(PROMPT_CFG, nb=bp, ls=seq, a_valid=PROMPT_CFG["a_lt"], b_valid=PROMPT_CFG["b_lt"])
    mem_rows = mem_prompt.reshape(bp * MEM_LEN, D_MODEL)
    zeros_in = (jnp.zeros((1, bp, SC_CONV_W - 1, SC_DIM), F32),
                jnp.zeros((1, bp, SSD_CONV_W - 1, SSD_CONV_DIM), F32),
                jnp.zeros((1, bp, SSD_HEADS, SSD_HEADDIM, SSD_STATE), F32))
    yp = x_prompt.reshape(bp * seq, D_MODEL)
    mk_p = mv_p = None
    st_p = (None, None, None)
    for l in range(DEPTH):
        mk_p = _matmul(mem_rows, wp["w_mem_kv"], l, 1024, 1024, col0=0, ncols=D_MODEL, stacked_prev=mk_p,
                       stacked=True)
        mv_p = _matmul(mem_rows, wp["w_mem_kv"], l, 1024, 1024, col0=D_MODEL, ncols=D_MODEL, stacked_prev=mv_p,
                       stacked=True)
        yp, st_p = _decoder_layer(l, yp, mk_p.reshape(DEPTH, bp, MEM_LEN, D_MODEL),
                                  mv_p.reshape(DEPTH, bp, MEM_LEN, D_MODEL), zeros_in, 0, st_p, wp, p, cfg_p)

    ls = SAMPLE_PAD
    cfg_s = dict(tm_proj=bs * ls, nb=bs, ls=ls, a_nseq=16, a_lt=ls, a_valid=dec_seq, b_nseq=4, b_lt=ls, b_q=ls,
                 b_valid=dec_seq, m_nseq=4, m_lt=ls, tm_merge=512, moe_tc=bs * ls)
    ys = jnp.pad(x_sample, ((0, 0), (0, ls - dec_seq), (0, 0))).reshape(bs * ls, D_MODEL)
    cache_k = cache_mem_k.reshape(DEPTH, bs, MEM_LEN, D_MODEL)
    cache_v = cache_mem_v.reshape(DEPTH, bs, MEM_LEN, D_MODEL)
    st_s = (None, None, None)
    for l in range(DEPTH):
        ys, st_s = _decoder_layer(l, ys, cache_k, cache_v, (state_sconv, state_ssd_conv, state_ssd), l, st_s,
                                  wp, p, cfg_s)

    y_prompt = yp.reshape(bp, seq, D_MODEL)
    y_sample = ys.reshape(bs, ls, D_MODEL)[:, :dec_seq]
    kv_shape = (DEPTH, bp, MEM_LEN, MEM_HEADS, MEM_HEAD_DIM)
    return (y_prompt, y_sample, mk_p.reshape(kv_shape), mv_p.reshape(kv_shape), st_p[0], st_p[1], st_p[2],
            st_s[0], st_s[1], st_s[2])
```

```python
import functools

import jax
import jax.numpy as jnp
from jax import lax
from jax.experimental import pallas as pl
from jax.experimental.pallas import tpu as pltpu

F32 = jnp.float32
BF16 = jnp.bfloat16

D_MODEL = 1024
DEPTH = 4
SC_DIM = D_MODEL
SC_CONV_W = 3
SSD_INNER = 2 * D_MODEL
SSD_HEADDIM = 64
SSD_HEADS = SSD_INNER // SSD_HEADDIM
SSD_GROUPS = 4
SSD_STATE = 128
SSD_CONV_W = 4
SSD_BC = SSD_GROUPS * SSD_STATE
SSD_CONV_DIM = SSD_INNER + 2 * SSD_BC
SSD_CHUNK = 128
HEADS_PER_GROUP = SSD_HEADS // SSD_GROUPS
GROUP_INNER = SSD_INNER // SSD_GROUPS
MEM_LEN = 256
MEM_HEADS = 4
MEM_HEAD_DIM = D_MODEL // MEM_HEADS
N_EXP_GROUPS = 4
EXP_PER_GROUP = 8
N_EXPERTS = N_EXP_GROUPS * EXP_PER_GROUP
EXP_TOPK = 2
EXP_FF = 512
MOE_BLOCK = 128
ALPHA = (2 * DEPTH) ** 0.25
LN_EPS = 1e-5
RMS_EPS = 1e-6

LANES = 128
SUBLANES = 8
VMEM_LIMIT = 56 * 1024 * 1024

COL_XBC = 0
COL_BCX = SSD_CONV_DIM
COL_Z = COL_BCX + 3 * SC_DIM
COL_Q = COL_Z + SSD_INNER
COL_G = COL_Q + D_MODEL
PROJ_COLS = COL_G + 3 * D_MODEL


def _dot(a, b):
    return jnp.dot(a, b, preferred_element_type=F32)


def _dot_exact(a, b):
    return jnp.dot(a, b, preferred_element_type=F32, precision=lax.Precision.HIGHEST)


def _split_dot(a, b, terms, dims=(((1,), (0,)), ((), ()))):
    acc = None
    rest = a
    for t in range(terms):
        piece = rest.astype(BF16)
        part = lax.dot_general(piece, b, dims, preferred_element_type=F32)
        acc = part if acc is None else acc + part
        if t + 1 < terms:
            rest = rest - piece.astype(F32)
    return acc


def _sigmoid(x):
    return 1.0 / (1.0 + jnp.exp(-x))


def _silu(x):
    return x * _sigmoid(x)


def _layer_norm(x, g, b):
    mu = jnp.mean(x, axis=-1, keepdims=True)
    xc = x - mu
    var = jnp.mean(xc * xc, axis=-1, keepdims=True)
    return xc * lax.rsqrt(var + LN_EPS) * g + b


def _params(*sem):
    return pltpu.CompilerParams(dimension_semantics=sem, vmem_limit_bytes=VMEM_LIMIT)


def _skip_alias_refs(kernel_fn, n_in, n_alias):
    def wrapped(*refs):
        return kernel_fn(*refs[:n_in], *refs[n_in + n_alias:])
    return wrapped


def _alias_args(prevs, n_in, first_out):
    prevs = [p for p in prevs if p is not None]
    specs = [pl.BlockSpec(memory_space=pl.ANY)] * len(prevs)
    aliases = {n_in + k: first_out + k for k in range(len(prevs))}
    return prevs, specs, aliases


def _mm_kernel(x_ref, w_ref, o_ref):
    o_ref[...] = _dot(x_ref[...].astype(BF16), w_ref[...])


def _matmul(x, w, layer, tm, tn):
    m, k = x.shape
    n = w.shape[2]
    tm = min(tm, m)
    tn = min(tn, n)
    return pl.pallas_call(
        _mm_kernel,
        grid=(m // tm, n // tn),
        in_specs=[pl.BlockSpec((tm, k), lambda i, j: (i, 0)),
                  pl.BlockSpec((None, k, tn), lambda i, j: (layer, 0, j))],
        out_specs=pl.BlockSpec((tm, tn), lambda i, j: (i, j)),
        out_shape=jax.ShapeDtypeStruct((m, n), F32),
        compiler_params=_params("parallel", "arbitrary"),
        name="matmul",
    )(x, w)


def _memkv_kernel(x_ref, w_ref, o_ref, *, nbt):
    res = _dot(x_ref[...].astype(BF16), w_ref[...])
    for h in range(MEM_HEADS):
        o_ref[:, :, h, :] = res[:, h * MEM_HEAD_DIM:(h + 1) * MEM_HEAD_DIM].reshape(nbt, MEM_LEN, MEM_HEAD_DIM)


def _memkv(mem_rows, w_kv, layer, which, stacked_prev, *, nb, nbt):
    prevs, pspecs, aliases = _alias_args([stacked_prev], 2, 0)
    kern = _skip_alias_refs(functools.partial(_memkv_kernel, nbt=nbt), 2, len(prevs))
    return pl.pallas_call(
        kern,
        grid=(nb // nbt,),
        in_specs=[pl.BlockSpec((nbt * MEM_LEN, D_MODEL), lambda i: (i, 0)),
                  pl.BlockSpec((None, D_MODEL, D_MODEL), lambda i: (layer, 0, which))] + pspecs,
        out_specs=pl.BlockSpec((None, nbt, MEM_LEN, MEM_HEADS, MEM_HEAD_DIM), lambda i: (layer, i, 0, 0, 0)),
        out_shape=jax.ShapeDtypeStruct((DEPTH, nb, MEM_LEN, MEM_HEADS, MEM_HEAD_DIM), F32),
        input_output_aliases=aliases,
        compiler_params=_params("parallel"),
        name="mem_kv",
    )(mem_rows, w_kv, *prevs)


def _shift_rows(u, tail, d):
    nseq, lt, c_dim = u.shape
    row = lax.broadcasted_iota(jnp.int32, (SUBLANES, c_dim), 0)
    outs = []
    for s in range(nseq):
        rolled = pltpu.roll(u[s], d, axis=0)
        first = jnp.where(row < d, pltpu.roll(tail[s], d, axis=0), rolled[0:SUBLANES])
        outs.append(first if lt == SUBLANES else jnp.concatenate([first, rolled[SUBLANES:]], axis=0))
    return jnp.stack(outs, axis=0)


def _causal_conv(u, prev_ref, w, st_ref, ext_ref, *, valid):
    nseq, lt, _ = u.shape
    width = w.shape[0]
    halo = width - 1

    @pl.when(pl.program_id(1) == 0)
    def _():
        ext_ref[:, 0:SUBLANES, :] = jnp.zeros_like(ext_ref[:, 0:SUBLANES, :])
        ext_ref[:, SUBLANES - halo:SUBLANES, :] = prev_ref[...]

    tail = ext_ref[:, 0:SUBLANES, :]
    last8 = u[:, lt - SUBLANES:lt, :]
    ext_ref[:, SUBLANES:2 * SUBLANES, :] = last8
    out = _shift_rows(u, tail, halo) * w[0:1, :]
    for k in range(1, halo):
        out = out + _shift_rows(u, tail, halo - k) * w[k:k + 1, :]
    out = out + u * w[halo:width, :]
    end = 2 * SUBLANES - (lt - valid)
    st_ref[...] = ext_ref[:, end - halo:end, :]
    ext_ref[:, 0:SUBLANES, :] = last8
    return out


def _branch_a_kernel(bcx_ref, g_ref, prev_ref, wc_ref, wout_ref, o_ref, st_ref, ext_ref, *, nseq, lt, valid):
    c_dim = SC_DIM
    rows = nseq * lt
    bcx = bcx_ref[...]
    scb = bcx[:, :c_dim]
    u = (bcx[:, c_dim:2 * c_dim] * bcx[:, 2 * c_dim:]).reshape(nseq, lt, c_dim)
    v = _causal_conv(u, prev_ref, wc_ref[...], st_ref, ext_ref, valid=valid)
    ya = _dot((scb * v.reshape(rows, c_dim)).astype(BF16), wout_ref[...])
    o_ref[...] = _sigmoid(g_ref[...]) * ya


def _branch_a(proj, prev, prev_layer, st_prev, w_conv, w_out, *, layer, nb, ls, nseq, lt, valid):
    nj = ls // lt
    rows = nseq * lt
    n_in = 5
    prevs, pspecs, aliases = _alias_args([st_prev], n_in, 1)
    kern = _skip_alias_refs(functools.partial(_branch_a_kernel, nseq=nseq, lt=lt, valid=valid), n_in, len(prevs))
    return pl.pallas_call(
        kern,
        grid=(nb // nseq, nj),
        in_specs=[pl.BlockSpec((rows, 3 * SC_DIM), lambda i, j: (i * nj + j, COL_BCX // (3 * SC_DIM))),
                  pl.BlockSpec((rows, D_MODEL), lambda i, j: (i * nj + j, COL_G // D_MODEL)),
                  pl.BlockSpec((None, nseq, SC_CONV_W - 1, SC_DIM), lambda i, j: (prev_layer, i, 0, 0)),
                  pl.BlockSpec((None, SC_CONV_W, SC_DIM), lambda i, j: (layer, 0, 0)),
                  pl.BlockSpec((None, SC_DIM, D_MODEL), lambda i, j: (layer, 0, 0))] + pspecs,
        out_specs=[pl.BlockSpec((rows, D_MODEL), lambda i, j: (i * nj + j, 0)),
                   pl.BlockSpec((None, nseq, SC_CONV_W - 1, SC_DIM), lambda i, j: (layer, i, 0, 0))],
        out_shape=[jax.ShapeDtypeStruct((nb * ls, D_MODEL), F32),
                   jax.ShapeDtypeStruct((DEPTH, nb, SC_CONV_W - 1, SC_DIM), F32)],
        scratch_shapes=[pltpu.VMEM((nseq, 2 * SUBLANES, SC_DIM), F32)],
        input_output_aliases=aliases,
        compiler_params=_params("arbitrary", "arbitrary"),
        name="branch_a",
    )(proj, proj, prev, w_conv, w_out, *prevs)


def _branch_m_kernel(q_ref, g_ref, k_ref, v_ref, wout_ref, o_ref, *, nseq, lt):
    rows = nseq * lt
    scale = MEM_HEAD_DIM ** -0.5
    qb = q_ref[...].astype(BF16).reshape(nseq, lt, D_MODEL)
    outs = []
    for h in range(MEM_HEADS):
        sl = slice(h * MEM_HEAD_DIM, (h + 1) * MEM_HEAD_DIM)
        kh = k_ref[:, :, h, :].astype(BF16)
        vh = v_ref[:, :, h, :].astype(BF16)
        s = jnp.einsum("sld,smd->slm", qb[:, :, sl], kh, preferred_element_type=F32) * scale
        s = s - jnp.max(s, axis=-1, keepdims=True)
        p = jnp.exp(s)
        p = p / jnp.sum(p, axis=-1, keepdims=True)
        outs.append(jnp.einsum("slm,smd->sld", p.astype(BF16), vh, preferred_element_type=F32))
    o = jnp.concatenate(outs, axis=-1).reshape(rows, D_MODEL)
    o_ref[...] = _sigmoid(g_ref[...]) * _dot(o.astype(BF16), wout_ref[...])


def _branch_m(proj, mem_k, mem_v, w_out, *, layer, nb, ls, nseq, lt):
    nj = ls // lt
    rows = nseq * lt
    kern = functools.partial(_branch_m_kernel, nseq=nseq, lt=lt)
    kv_spec = pl.BlockSpec((None, nseq, MEM_LEN, MEM_HEADS, MEM_HEAD_DIM), lambda i, j: (layer, i, 0, 0, 0))
    return pl.pallas_call(
        kern,
        grid=(nb // nseq, nj),
        in_specs=[pl.BlockSpec((rows, D_MODEL), lambda i, j: (i * nj + j, COL_Q // D_MODEL)),
                  pl.BlockSpec((rows, D_MODEL), lambda i, j: (i * nj + j, COL_G // D_MODEL + 2)),
                  kv_spec, kv_spec,
                  pl.BlockSpec((None, D_MODEL, D_MODEL), lambda i, j: (layer, 0, 0))],
        out_specs=pl.BlockSpec((rows, D_MODEL), lambda i, j: (i * nj + j, 0)),
        out_shape=jax.ShapeDtypeStruct((nb * ls, D_MODEL), F32),
        compiler_params=_params("parallel", "arbitrary"),
        name="branch_m",
    )(proj, proj, mem_k, mem_v, w_out)


def _softplus(x):
    return jnp.maximum(x, 0.0) + jnp.log1p(jnp.exp(-jnp.abs(x)))


def _ssd_chunk(s, c, xc_ref, z_ref, dt_ref, h_ref, yn_ref, a_row, d_row, nw_row, expand, *, lt, q):
    r0 = pl.multiple_of(c * q, q)
    row0 = pl.multiple_of(s * lt + c * q, q)
    xs = xc_ref[s, pl.ds(r0, q), 0:SSD_INNER]
    bm = xc_ref[s, pl.ds(r0, q), SSD_INNER:SSD_INNER + SSD_BC].astype(BF16)
    cm = xc_ref[s, pl.ds(r0, q), SSD_INNER + SSD_BC:SSD_CONV_DIM].astype(BF16)
    dt = dt_ref[pl.ds(row0, q), :]
    da = dt * a_row
    ti = lax.broadcasted_iota(jnp.int32, (q, q), 0)
    si = lax.broadcasted_iota(jnp.int32, (q, q), 1)
    causal = si <= ti
    tri = jnp.where(causal, 1.0, 0.0).astype(BF16)
    acs = _tri_cumsum(tri, da)
    acs_t = _split_dot(da, tri, 3, dims=(((0,), (1,)), ((), ())))
    acs_last = acs[q - 1:q, :]
    eacs = jnp.exp(acs)
    eacs_hi = eacs.astype(BF16).astype(F32)
    stack = jnp.concatenate([dt, jnp.exp(acs_last - acs), eacs_hi, eacs - eacs_hi], axis=0)
    stack_e = _dot(stack.astype(BF16), expand)
    dt_e, dte_e = stack_e[0:q], stack_e[q:2 * q]
    eacs_e = stack_e[2 * q:3 * q] + stack_e[3 * q:4 * q]
    xdt = xs * dt_e
    xdt_b = xdt.astype(BF16)
    xdte_b = (xdt * dte_e).astype(BF16)
    dec_full = jnp.broadcast_to(jnp.exp(acs_t[:, q - 1:q]), (LANES, SSD_STATE))
    lane = lax.broadcasted_iota(jnp.int32, (q, LANES), 1)
    lo_mask = lane < SSD_HEADDIM
    y_groups = []
    for g in range(SSD_GROUPS):
        gs = slice(g * SSD_STATE, (g + 1) * SSD_STATE)
        gi = slice(g * GROUP_INNER, (g + 1) * GROUP_INNER)
        cb = lax.dot_general(cm[:, gs], bm[:, gs], (((1,), (1,)), ((), ())), preferred_element_type=F32)
        h0 = g * HEADS_PER_GROUP
        hg = h_ref[s, h0:h0 + HEADS_PER_GROUP].reshape(GROUP_INNER, SSD_STATE)
        y_int = lax.dot_general(cm[:, gs], hg.astype(BF16), (((1,), (1,)), ((), ())),
                                preferred_element_type=F32) * eacs_e[:, gi]
        upd = lax.dot_general(xdte_b[:, gi], bm[:, gs], (((0,), (0,)), ((), ())), preferred_element_type=F32)
        y_pairs = []
        for pr in range(HEADS_PER_GROUP // 2):
            col = g * GROUP_INNER + pr * LANES
            xpair = xdt_b[:, col:col + LANES]
            zero = jnp.zeros_like(xpair)
            masks = []
            for hh in range(2):
                h = h0 + 2 * pr + hh
                seg = acs[:, h:h + 1] - acs_t[h:h + 1, :]
                lm = jnp.exp(jnp.where(causal, seg, -jnp.inf))
                masks.append((cb * lm).astype(BF16))
            x_lo = jnp.where(lo_mask, xpair, zero)
            x_hi = jnp.where(lo_mask, zero, xpair)
            if q % LANES == 0:
                y_pairs.append(_dot(jnp.concatenate(masks, axis=1), jnp.concatenate([x_lo, x_hi], axis=0)))
            else:
                y_pairs.append(_dot(masks[0], x_lo) + _dot(masks[1], x_hi))
        y_groups.append(jnp.concatenate(y_pairs, axis=-1) + y_int)
        for hh in range(HEADS_PER_GROUP):
            h = h0 + hh
            h_ref[s, h] = h_ref[s, h] * dec_full[h:h + 1, :] + upd[hh * SSD_HEADDIM:(hh + 1) * SSD_HEADDIM, :]
    y = jnp.concatenate(y_groups, axis=-1) + d_row * xs
    yf = y * _silu(z_ref[pl.ds(row0, q), :])
    normed = []
    for g in range(SSD_GROUPS):
        yg = yf[:, g * GROUP_INNER:(g + 1) * GROUP_INNER]
        normed.append(yg * lax.rsqrt(jnp.mean(yg * yg, axis=-1, keepdims=True) + RMS_EPS))
    yn_ref[pl.ds(row0, q), :] = jnp.concatenate(normed, axis=-1) * nw_row


def _tri_cumsum(tri, da):
    acc = None
    rest = da
    for t in range(3):
        piece = rest.astype(BF16)
        part = _dot(tri, piece)
        acc = part if acc is None else acc + part
        if t < 2:
            rest = rest - piece.astype(F32)
    return acc


def _branch_b_kernel(xbc_ref, z_ref, dtr_ref, g_ref, cprev_ref, h0_ref, wc_ref, bc_ref, dtb_ref, alog_ref,
                     d_ref, nw_ref, exp_ref, wout_ref, o_ref, cst_ref, h_ref, ext_ref, xc_ref, dt_ref, yn_ref,
                     *, nseq, lt, q, valid):
    rows = nseq * lt

    @pl.when(pl.program_id(1) == 0)
    def _():
        h_ref[...] = h0_ref[...]

    conv = _causal_conv(xbc_ref[...].reshape(nseq, lt, SSD_CONV_DIM), cprev_ref, wc_ref[...], cst_ref, ext_ref,
                        valid=valid)
    xc_ref[...] = _silu(conv + bc_ref[...])
    dt = _softplus(dtr_ref[...] + dtb_ref[...])
    if valid < lt:
        t_in_seq = lax.broadcasted_iota(jnp.int32, (nseq, lt, LANES), 1).reshape(rows, LANES)
        dt = jnp.where(t_in_seq < valid, dt, 0.0)
    dt_ref[...] = dt
    a_row = -jnp.exp(alog_ref[...])
    d_row = d_ref[...]
    nw_row = nw_ref[...]
    expand = exp_ref[...]
    nchunk = lt // q

    def body(n, carry):
        _ssd_chunk(n // nchunk, n % nchunk, xc_ref, z_ref, dt_ref, h_ref, yn_ref, a_row, d_row, nw_row, expand,
                   lt=lt, q=q)
        return carry

    lax.fori_loop(0, nseq * nchunk, body, 0)
    o_ref[...] = _sigmoid(g_ref[...]) * _dot(yn_ref[...].astype(BF16), wout_ref[...])


def _branch_b(proj, dtr, cprev, h0, prev_layer, cst_prev, hst_prev, w_conv, b_conv, dt_bias, a_log, d_exp, norm_w,
              expand, w_out, *, layer, nb, ls, nseq, lt, q, valid):
    nj = ls // lt
    rows = nseq * lt
    n_in = 14
    prevs, pspecs, aliases = _alias_args([cst_prev, hst_prev], n_in, 1)
    kern = _skip_alias_refs(functools.partial(_branch_b_kernel, nseq=nseq, lt=lt, q=q, valid=valid),
                            n_in, len(prevs))
    row_blk = lambda i, j: (i * nj + j, 0)
    lay3 = lambda i, j: (layer, 0, 0)
    return pl.pallas_call(
        kern,
        grid=(nb // nseq, nj),
        in_specs=[pl.BlockSpec((rows, SSD_CONV_DIM), lambda i, j: (i * nj + j, COL_XBC // SSD_CONV_DIM)),
                  pl.BlockSpec((rows, SSD_INNER), lambda i, j: (i * nj + j, COL_Z // SSD_INNER)),
                  pl.BlockSpec((rows, LANES), row_blk),
                  pl.BlockSpec((rows, D_MODEL), lambda i, j: (i * nj + j, COL_G // D_MODEL + 1)),
                  pl.BlockSpec((None, nseq, SSD_CONV_W - 1, SSD_CONV_DIM), lambda i, j: (prev_layer, i, 0, 0)),
                  pl.BlockSpec((None, nseq, SSD_HEADS, SSD_HEADDIM, SSD_STATE),
                               lambda i, j: (prev_layer, i, 0, 0, 0)),
                  pl.BlockSpec((None, SSD_CONV_W, SSD_CONV_DIM), lay3),
                  pl.BlockSpec((None, 1, SSD_CONV_DIM), lay3),
                  pl.BlockSpec((None, 1, LANES), lay3),
                  pl.BlockSpec((None, 1, LANES), lay3),
                  pl.BlockSpec((None, 1, SSD_INNER), lay3),
                  pl.BlockSpec((None, 1, SSD_INNER), lay3),
                  pl.BlockSpec((LANES, SSD_INNER), lambda i, j: (0, 0)),
                  pl.BlockSpec((None, SSD_INNER, D_MODEL), lay3)] + pspecs,
        out_specs=[pl.BlockSpec((rows, D_MODEL), row_blk),
                   pl.BlockSpec((None, nseq, SSD_CONV_W - 1, SSD_CONV_DIM), lambda i, j: (layer, i, 0, 0)),
                   pl.BlockSpec((None, nseq, SSD_HEADS, SSD_HEADDIM, SSD_STATE),
                                lambda i, j: (layer, i, 0, 0, 0))],
        out_shape=[jax.ShapeDtypeStruct((nb * ls, D_MODEL), F32),
                   jax.ShapeDtypeStruct((DEPTH, nb, SSD_CONV_W - 1, SSD_CONV_DIM), F32),
                   jax.ShapeDtypeStruct((DEPTH, nb, SSD_HEADS, SSD_HEADDIM, SSD_STATE), F32)],
        scratch_shapes=[pltpu.VMEM((nseq, 2 * SUBLANES, SSD_CONV_DIM), F32),
                        pltpu.VMEM((nseq, lt, SSD_CONV_DIM), F32),
                        pltpu.VMEM((rows, LANES), F32),
                        pltpu.VMEM((rows, SSD_INNER), F32)],
        input_output_aliases=aliases,
        compiler_params=_params("arbitrary", "arbitrary"),
        name="branch_b",
    )(proj, proj, dtr, proj, cprev, h0, w_conv, b_conv, dt_bias, a_log, d_exp, norm_w, expand, w_out, *prevs)


def _merge_kernel(ya_ref, yb_ref, ym_ref, x_ref, wo_ref, g_ref, b_ref, wr_ref, br_ref, x1_ref, route_ref):
    merged = (ya_ref[...] + yb_ref[...]) + ym_ref[...]
    mix = _dot(merged.astype(BF16), wo_ref[...])
    x1 = _layer_norm(ALPHA * x_ref[...] + mix, g_ref[...], b_ref[...])
    x1_ref[...] = x1
    logits = _dot_exact(x1, wr_ref[...]) + br_ref[...]
    lane = lax.broadcasted_iota(jnp.int32, logits.shape, 1).astype(F32)
    neg = -jnp.inf
    big = float(LANES)
    is_grp = lane < N_EXP_GROUPS
    gl = jnp.where(is_grp, logits, neg)
    gmax = jnp.max(gl, axis=-1, keepdims=True)
    gsum = jnp.sum(jnp.exp(gl - gmax), axis=-1, keepdims=True)
    g_idx = jnp.min(jnp.where(jnp.logical_and(is_grp, gl == gmax), lane, big), axis=-1, keepdims=True)
    g_w = 1.0 / gsum
    e_lo = N_EXP_GROUPS + g_idx * EXP_PER_GROUP
    in_grp = jnp.logical_and(lane >= e_lo, lane < e_lo + EXP_PER_GROUP)
    el = jnp.where(in_grp, logits, neg)
    emax = jnp.max(el, axis=-1, keepdims=True)
    ee = jnp.exp(el - emax)
    esum = jnp.sum(ee, axis=-1, keepdims=True)
    i0 = jnp.min(jnp.where(jnp.logical_and(in_grp, el == emax), lane, big), axis=-1, keepdims=True)
    el2 = jnp.where(lane == i0, neg, el)
    emax2 = jnp.max(el2, axis=-1, keepdims=True)
    i1 = jnp.min(jnp.where(jnp.logical_and(in_grp, el2 == emax2), lane, big), axis=-1, keepdims=True)
    w0 = 1.0 / esum
    w1 = jnp.exp(emax2 - emax) / esum
    wsum = w0 + w1
    gate0 = g_w * w0 / wsum
    gate1 = g_w * w1 / wsum
    id0 = i0 - N_EXP_GROUPS
    id1 = i1 - N_EXP_GROUPS
    route = jnp.where(lane == 0, id0, jnp.where(lane == 1, id1, jnp.where(lane == 2, gate0,
                      jnp.where(lane == 3, gate1, 0.0))))
    route_ref[...] = route


def _merge(ya, yb, ym, x, w_o, ln_g, ln_b, w_r, b_r, *, layer, tm):
    t = x.shape[0]
    row = lambda i: (i, 0)
    lay = lambda i: (layer, 0, 0)
    return pl.pallas_call(
        _merge_kernel,
        grid=(t // tm,),
        in_specs=[pl.BlockSpec((tm, D_MODEL), row)] * 4 + [
            pl.BlockSpec((None, D_MODEL, D_MODEL), lay),
            pl.BlockSpec((None, 1, D_MODEL), lay), pl.BlockSpec((None, 1, D_MODEL), lay),
            pl.BlockSpec((None, D_MODEL, LANES), lay), pl.BlockSpec((None, 1, LANES), lay)],
        out_specs=[pl.BlockSpec((tm, D_MODEL), row), pl.BlockSpec((tm, LANES), row)],
        out_shape=[jax.ShapeDtypeStruct((t, D_MODEL), F32), jax.ShapeDtypeStruct((t, LANES), F32)],
        compiler_params=_params("parallel"),
        name="merge_ln_router",
    )(ya, yb, ym, x, w_o, ln_g, ln_b, w_r, b_r)


def _moe_kernel(bexp_ref, nused_ref, starts_ref, pstarts_ref, counts_ref, tok_ref, row_ref, gate_ref, x_ref,
                win_ref, wout_ref, g_ref, b_ref, o_ref, xg_ref, eo_ref, y2_ref, *, nblk, tc):
    c = pl.program_id(0)
    i = pl.program_id(1)

    @pl.when(i < nused_ref[c])
    def _():
        tbl = c * N_EXPERTS + bexp_ref[c * nblk + i]
        off0 = i * MOE_BLOCK - pstarts_ref[tbl]
        pos0 = starts_ref[tbl] + off0
        nvalid = counts_ref[tbl] - off0

        for r in range(MOE_BLOCK):
            tok = tok_ref[0, 0, pos0 + r]
            xg_ref[r:r + 1, :] = x_ref[pl.ds(tok, 1), :]
        gu = _dot(xg_ref[...].astype(BF16), win_ref[...])
        hid = _silu(gu[:, :EXP_FF]) * gu[:, EXP_FF:]
        eo_ref[...] = _dot(hid.astype(BF16), wout_ref[...])
        for r in range(MOE_BLOCK):
            row = jnp.where(r < nvalid, row_ref[0, 0, pos0 + r], EXP_TOPK * tc)
            y2_ref[pl.ds(row, 1), :] = gate_ref[0, 0, pos0 + r] * eo_ref[r:r + 1, :]

    @pl.when(i == nblk - 1)
    def _():
        ffn = y2_ref[0:tc, :] + y2_ref[tc:2 * tc, :]
        o_ref[...] = _layer_norm(ALPHA * x_ref[...] + ffn, g_ref[...], b_ref[...])


def _moe(x1, tables, slot_arrays, w_in, w_out, ln_g, ln_b, *, layer, tc, nblk):
    t = x1.shape[0]
    nchunks = t // tc
    s = tc * EXP_TOPK
    kern = functools.partial(_moe_kernel, nblk=nblk, tc=tc)
    smem_row = lambda c, i, *_: (c, 0, 0)
    lay = lambda c, i, *_: (layer, 0, 0)
    expert = lambda c, i, be, *_: (layer, be[c * nblk + i], 0, 0)
    grid_spec = pltpu.PrefetchScalarGridSpec(
        num_scalar_prefetch=5,
        grid=(nchunks, nblk),
        in_specs=[pl.BlockSpec((1, 1, s + MOE_BLOCK), smem_row, memory_space=pltpu.SMEM)] * 3 + [
                  pl.BlockSpec((tc, D_MODEL), lambda c, i, *_: (c, 0), pipeline_mode=pl.Buffered(1)),
                  pl.BlockSpec((None, None, D_MODEL, 2 * EXP_FF), expert),
                  pl.BlockSpec((None, None, EXP_FF, D_MODEL), expert),
                  pl.BlockSpec((None, 1, D_MODEL), lay),
                  pl.BlockSpec((None, 1, D_MODEL), lay)],
        out_specs=pl.BlockSpec((tc, D_MODEL), lambda c, i, *_: (c, 0)),
        scratch_shapes=[pltpu.VMEM((MOE_BLOCK, D_MODEL), F32), pltpu.VMEM((MOE_BLOCK, D_MODEL), F32),
                        pltpu.VMEM((EXP_TOPK * tc + SUBLANES, D_MODEL), F32)],
    )
    return pl.pallas_call(
        kern,
        grid_spec=grid_spec,
        out_shape=jax.ShapeDtypeStruct((t, D_MODEL), F32),
        compiler_params=_params("arbitrary", "arbitrary"),
        name="moe_experts",
    )(*tables, *[a[:, None, :] for a in slot_arrays], x1, w_in, w_out, ln_g, ln_b)


def _dispatch(route, *, tc, nblk):
    t = route.shape[0]
    nchunks = t // tc
    s = tc * EXP_TOPK
    ids = route[:, 0:EXP_TOPK].astype(jnp.int32).reshape(nchunks, s)
    gates = route[:, EXP_TOPK:2 * EXP_TOPK].reshape(nchunks, s)
    slot = jnp.broadcast_to(jnp.arange(s, dtype=jnp.int32)[None, :], (nchunks, s))
    _, order, gate_sorted = lax.sort((ids, slot, gates), dimension=1, is_stable=True, num_keys=1)
    tok_sorted = order // EXP_TOPK
    row_sorted = (order % EXP_TOPK) * tc + tok_sorted
    counts =jnp.sum((ids[:, :, None] == jnp.arange(N_EXPERTS, dtype=jnp.int32)).astype(jnp.int32), axis=1)
    starts = jnp.cumsum(counts, axis=-1) - counts
    padded = (counts + MOE_BLOCK - 1) // MOE_BLOCK * MOE_BLOCK
    pends = jnp.cumsum(padded, axis=-1)
    pstarts = pends - padded
    blk_start = jnp.arange(nblk, dtype=jnp.int32) * MOE_BLOCK
    bexp = jnp.sum((pends[:, None, :] <= blk_start[None, :, None]).astype(jnp.int32), axis=-1)
    bexp = jnp.minimum(bexp, N_EXPERTS - 1).reshape(-1)
    nused = pends[:, -1] // MOE_BLOCK
    tables = (bexp, nused, starts.reshape(-1), pstarts.reshape(-1), counts.reshape(-1))
    pad = ((0, 0), (0, MOE_BLOCK))
    return tables, (jnp.pad(tok_sorted, pad), jnp.pad(row_sorted, pad), jnp.pad(gate_sorted, pad))


def _prep_weights(w_in, w_sconv_out, w_ssd_out, w_mem_k, w_mem_v, w_mem_out, w_o, w_exp_in, w_exp_out,
                  w_router_grp, b_router_grp, w_router_exp, b_router_exp, ssd_dt_bias, ssd_a_log, ssd_d):
    o_b, o_z, o_xbc, o_dt, o_q, o_g = 0, 3072, 5120, 8192, 8224, 9248
    w_main = jnp.concatenate([w_in[:, :, o_xbc:o_dt], w_in[:, :, o_b:o_z], w_in[:, :, o_z:o_xbc],
                              w_in[:, :, o_q:o_g], w_in[:, :, o_g:]], axis=-1).astype(BF16)
    pad_h = LANES - SSD_HEADS
    w_dt = jnp.pad(w_in[:, :, o_dt:o_q], ((0, 0), (0, 0), (0, pad_h))).astype(BF16)
    pad_r = LANES - N_EXP_GROUPS - N_EXPERTS
    w_r = jnp.pad(jnp.concatenate([w_router_grp, w_router_exp], axis=-1), ((0, 0), (0, 0), (0, pad_r)))
    b_r = jnp.pad(jnp.concatenate([b_router_grp, b_router_exp], axis=-1), ((0, 0), (0, pad_r)))[:, None, :]
    head_of_col = jnp.arange(SSD_INNER, dtype=jnp.int32) // SSD_HEADDIM
    expand = (jnp.arange(LANES, dtype=jnp.int32)[:, None] == head_of_col[None, :]).astype(BF16)
    return dict(
        w_main=w_main, w_dt=w_dt, w_r=w_r, b_r=b_r, expand=expand,
        w_sconv_out=w_sconv_out.astype(BF16), w_ssd_out=w_ssd_out.astype(BF16),
        w_mem_kv=jnp.concatenate([w_mem_k, w_mem_v], axis=-1).astype(BF16),
        w_mem_out=w_mem_out.astype(BF16), w_o=w_o.astype(BF16),
        w_exp_in=w_exp_in.astype(BF16), w_exp_out=w_exp_out.astype(BF16),
        dt_bias=jnp.pad(ssd_dt_bias, ((0, 0), (0, pad_h)))[:, None, :],
        a_log=jnp.pad(ssd_a_log, ((0, 0), (0, pad_h)))[:, None, :],
        d_exp=jnp.repeat(ssd_d, SSD_HEADDIM, axis=-1)[:, None, :],
    )


def _moe_blocks(tc):
    return (tc * EXP_TOPK + N_EXPERTS * (MOE_BLOCK - 1) + MOE_BLOCK - 1) // MOE_BLOCK


def _decoder_layer(l, x, mem_k, mem_v, states_in, prev_layer, states_out, wp, p, cfg):
    nb, ls = cfg["nb"], cfg["ls"]
    sc_in, cv_in, h_in = states_in
    sc_out, cv_out, h_out = states_out
    proj = _matmul(x, wp["w_main"], l, cfg["tm_proj"], 1024)
    dtr = _matmul(x, wp["w_dt"], l, cfg["tm_proj"], LANES)
    ya, sc_new = _branch_a(proj, sc_in, prev_layer, sc_out, p["w_sconv"], wp["w_sconv_out"],
                           layer=l, nb=nb, ls=ls, nseq=cfg["a_nseq"], lt=cfg["a_lt"], valid=cfg["a_valid"])
    yb, cv_new, h_new = _branch_b(proj, dtr, cv_in, h_in, prev_layer, cv_out, h_out, p["w_ssd_conv"],
                                  p["b_ssd_conv"], wp["dt_bias"], wp["a_log"], wp["d_exp"], p["ssd_norm_w"],
                                  wp["expand"], wp["w_ssd_out"],
                                  layer=l, nb=nb, ls=ls, nseq=cfg["b_nseq"], lt=cfg["b_lt"], q=cfg["b_q"],
                                  valid=cfg["b_valid"])
    ym = _branch_m(proj, mem_k, mem_v, wp["w_mem_out"], layer=l, nb=nb, ls=ls, nseq=cfg["m_nseq"],
                   lt=cfg["m_lt"])
    x1, route = _merge(ya, yb, ym, x, wp["w_o"], p["ln1_g"], p["ln1_b"], wp["w_r"], wp["b_r"],
                       layer=l, tm=cfg["tm_merge"])
    tc = cfg["moe_tc"]
    nblk = _moe_blocks(tc)
    tables, slot_arrays = _dispatch(route, tc=tc, nblk=nblk)
    x2 = _moe(x1, tables, slot_arrays, wp["w_exp_in"], wp["w_exp_out"], p["ln2_g"], p["ln2_b"],
              layer=l, tc=tc, nblk=nblk)
    return x2, (sc_new, cv_new, h_new)


PROMPT_CFG = dict(tm_proj=1024, a_nseq=1, a_lt=512, b_nseq=1, b_lt=256, b_q=SSD_CHUNK, m_nseq=1, m_lt=512,
                  tm_merge=512, moe_tc=2048)
SAMPLE_PAD = SUBLANES


def kernel(x_prompt, x_sample, mem_prompt, cache_mem_k, cache_mem_v, state_sconv, state_ssd_conv, state_ssd, w_in, w_sconv, w_sconv_out, w_ssd_conv, b_ssd_conv, ssd_dt_bias, ssd_a_log, ssd_d, ssd_norm_w, w_ssd_out, w_mem_k, w_mem_v, w_mem_out, w_o, ln1_g, ln1_b, w_router_grp, b_router_grp, w_router_exp, b_router_exp, w_exp_in, w_exp_out, ln2_g, ln2_b):
    bp, seq, _ = x_prompt.shape
    bs, dec_seq, _ = x_sample.shape
    wp = _prep_weights(w_in, w_sconv_out, w_ssd_out, w_mem_k, w_mem_v, w_mem_out, w_o, w_exp_in, w_exp_out,
                       w_router_grp, b_router_grp, w_router_exp, b_router_exp, ssd_dt_bias, ssd_a_log, ssd_d)
    p = dict(w_sconv=w_sconv, w_ssd_conv=w_ssd_conv, b_ssd_conv=b_ssd_conv[:, None, :],
             ssd_norm_w=ssd_norm_w[:, None, :], ln1_g=ln1_g[:, None, :], ln1_b=ln1_b[:, None, :],
             ln2_g=ln2_g[:, None, :], ln2_b=ln2_b[:, None, :])

    cfg_p = dict(PROMPT_CFG, nb=bp, ls=seq, a_valid=PROMPT_CFG["a_lt"], b_valid=PROMPT_CFG["b_lt"])
    mem_rows = mem_prompt.reshape(bp * MEM_LEN, D_MODEL)
    zeros_in = (jnp.zeros((1, bp, SC_CONV_W - 1, SC_DIM), F32),
                jnp.zeros((1, bp, SSD_CONV_W - 1, SSD_CONV_DIM), F32),
                jnp.zeros((1, bp, SSD_HEADS, SSD_HEADDIM, SSD_STATE), F32))
    yp = x_prompt.reshape(bp * seq, D_MODEL)
    mk_p = mv_p = None
    st_p = (None, None, None)
    for l in range(DEPTH):
        mk_p = _memkv(mem_rows, wp["w_mem_kv"], l, 0, mk_p, nb=bp, nbt=4)
        mv_p = _memkv(mem_rows, wp["w_mem_kv"], l, 1, mv_p, nb=bp, nbt=4)
        yp, st_p = _decoder_layer(l, yp, mk_p, mv_p, zeros_in, 0, st_p, wp, p, cfg_p)

    ls = SAMPLE_PAD
    cfg_s = dict(tm_proj=bs * ls, nb=bs, ls=ls, a_nseq=16, a_lt=ls, a_valid=dec_seq, b_nseq=4, b_lt=ls, b_q=ls,
                 b_valid=dec_seq, m_nseq=2, m_lt=ls, tm_merge=512, moe_tc=bs * ls)
    ys = jnp.pad(x_sample, ((0, 0), (0, ls - dec_seq), (0, 0))).reshape(bs * ls, D_MODEL)
    st_s = (None, None, None)
    for l in range(DEPTH):
        ys, st_s = _decoder_layer(l, ys, cache_mem_k, cache_mem_v, (state_sconv, state_ssd_conv, state_ssd), l,
                                  st_s, wp, p, cfg_s)

    y_prompt = yp.reshape(bp, seq, D_MODEL)
    y_sample = ys.reshape(bs, ls, D_MODEL)[:, :dec_seq]
    return (y_prompt, y_sample, mk_p, mv_p, st_p[0], st_p[1], st_p[2], st_s[0], st_s[1], st_s[2])
```

```python
import functools

import jax
import jax.numpy as jnp
from jax import lax
from jax.experimental import pallas as pl
from jax.experimental.pallas import tpu as pltpu

F32 = jnp.float32
BF16 = jnp.bfloat16

D_MODEL = 1024
DEPTH = 4
SC_DIM = D_MODEL
SC_CONV_W = 3
SSD_INNER = 2 * D_MODEL
SSD_HEADDIM = 64
SSD_HEADS = SSD_INNER // SSD_HEADDIM
SSD_GROUPS = 4
SSD_STATE = 128
SSD_CONV_W = 4
SSD_BC = SSD_GROUPS * SSD_STATE
SSD_CONV_DIM = SSD_INNER + 2 * SSD_BC
SSD_CHUNK = 128
HEADS_PER_GROUP = SSD_HEADS // SSD_GROUPS
GROUP_INNER = SSD_INNER // SSD_GROUPS
MEM_LEN = 256
MEM_HEADS = 4
MEM_HEAD_DIM = D_MODEL // MEM_HEADS
N_EXP_GROUPS = 4
EXP_PER_GROUP = 8
N_EXPERTS = N_EXP_GROUPS * EXP_PER_GROUP
EXP_TOPK = 2
EXP_FF = 512
MOE_BLOCK = 128
ALPHA = (2 * DEPTH) ** 0.25
LN_EPS = 1e-5
RMS_EPS = 1e-6

LANES = 128
SUBLANES = 8
VMEM_LIMIT = 56 * 1024 * 1024

COL_XBC = 0
COL_BCX = SSD_CONV_DIM
COL_Z = COL_BCX + 3 * SC_DIM
COL_Q = COL_Z + SSD_INNER
COL_G = COL_Q + D_MODEL
PROJ_COLS = COL_G + 3 * D_MODEL


def _dot(a, b):
    return jnp.dot(a, b, preferred_element_type=F32)


def _split_dot(a, b, terms, dims=(((1,), (0,)), ((), ()))):
    acc = None
    rest = a
    for t in range(terms):
        piece = rest.astype(BF16)
        part = lax.dot_general(piece, b, dims, preferred_element_type=F32)
        acc = part if acc is None else acc + part
        if t + 1 < terms:
            rest = rest - piece.astype(F32)
    return acc


def _sigmoid(x):
    return 1.0 / (1.0 + jnp.exp(-x))


def _silu(x):
    return x * _sigmoid(x)


def _layer_norm(x, g, b):
    mu = jnp.mean(x, axis=-1, keepdims=True)
    xc = x - mu
    var = jnp.mean(xc * xc, axis=-1, keepdims=True)
    return xc * lax.rsqrt(var + LN_EPS) * g + b


def _params(*sem):
    return pltpu.CompilerParams(dimension_semantics=sem, vmem_limit_bytes=VMEM_LIMIT)


def _skip_alias_refs(kernel_fn, n_in, n_alias):
    def wrapped(*refs):
        return kernel_fn(*refs[:n_in], *refs[n_in + n_alias:])
    return wrapped


def _alias_args(prevs, n_in, first_out):
    prevs = [p for p in prevs if p is not None]
    specs = [pl.BlockSpec(memory_space=pl.ANY)] * len(prevs)
    aliases = {n_in + k: first_out + k for k in range(len(prevs))}
    return prevs, specs, aliases


def _mm_kernel(x_ref, w_ref, o_ref):
    o_ref[...] = _dot(x_ref[...].astype(BF16), w_ref[...])


def _matmul(x, w, layer, tm, tn):
    m, k = x.shape
    n = w.shape[2]
    tm = min(tm, m)
    tn = min(tn, n)
    return pl.pallas_call(
        _mm_kernel,
        grid=(m // tm, n // tn),
        in_specs=[pl.BlockSpec((tm, k), lambda i, j: (i, 0)),
                  pl.BlockSpec((None, k, tn), lambda i, j: (layer, 0, j))],
        out_specs=pl.BlockSpec((tm, tn), lambda i, j: (i, j)),
        out_shape=jax.ShapeDtypeStruct((m, n), F32),
        compiler_params=_params("parallel", "arbitrary"),
        name="matmul",
    )(x, w)


def _memkv_kernel(x_ref, w_ref, o_ref, *, nbt):
    res = _dot(x_ref[...].astype(BF16), w_ref[...])
    for h in range(MEM_HEADS):
        o_ref[:, :, h, :] = res[:, h * MEM_HEAD_DIM:(h + 1) * MEM_HEAD_DIM].reshape(nbt, MEM_LEN, MEM_HEAD_DIM)


def _memkv(mem_rows, w_kv, layer, which, stacked_prev, *, nb, nbt):
    prevs, pspecs, aliases = _alias_args([stacked_prev], 2, 0)
    kern = _skip_alias_refs(functools.partial(_memkv_kernel, nbt=nbt), 2, len(prevs))
    return pl.pallas_call(
        kern,
        grid=(nb // nbt,),
        in_specs=[pl.BlockSpec((nbt * MEM_LEN, D_MODEL), lambda i: (i, 0)),
                  pl.BlockSpec((None, D_MODEL, D_MODEL), lambda i: (layer, 0, which))] + pspecs,
        out_specs=pl.BlockSpec((None, nbt, MEM_LEN, MEM_HEADS, MEM_HEAD_DIM), lambda i: (layer, i, 0, 0, 0)),
        out_shape=jax.ShapeDtypeStruct((DEPTH, nb, MEM_LEN, MEM_HEADS, MEM_HEAD_DIM), F32),
        input_output_aliases=aliases,
        compiler_params=_params("parallel"),
        name="mem_kv",
    )(mem_rows, w_kv, *prevs)


def _shift_rows(u, tail, d):
    nseq, lt, c_dim = u.shape
    groups = lt // SUBLANES
    row = lax.broadcasted_iota(jnp.int32, (SUBLANES, c_dim), 0)
    outs = []
    for s in range(nseq):
        rot = pltpu.roll(u[s].reshape(groups, SUBLANES, c_dim), d, axis=1)
        before = pltpu.roll(tail[s], d, axis=0)[None]
        if groups > 1:
            before = jnp.concatenate([before, rot[:groups - 1]], axis=0)
        outs.append(jnp.where(row < d, before, rot).reshape(lt, c_dim))
    return jnp.stack(outs, axis=0)


def _causal_conv(u, prev_ref, w, st_ref, ext_ref, *, valid):
    nseq, lt, _ = u.shape
    width = w.shape[0]
    halo = width - 1

    @pl.when(pl.program_id(1) == 0)
    def _():
        ext_ref[:, 0:SUBLANES, :] = jnp.zeros_like(ext_ref[:, 0:SUBLANES, :])
        ext_ref[:, SUBLANES - halo:SUBLANES, :] = prev_ref[...]

    tail = ext_ref[:, 0:SUBLANES, :]
    last8 = u[:, lt - SUBLANES:lt, :]
    ext_ref[:, SUBLANES:2 * SUBLANES, :] = last8
    out = _shift_rows(u, tail, halo) * w[0:1, :]
    for k in range(1, halo):
        out = out + _shift_rows(u, tail, halo - k) * w[k:k + 1, :]
    out = out + u * w[halo:width, :]
    end = 2 * SUBLANES - (lt - valid)
    st_ref[...] = ext_ref[:, end - halo:end, :]
    ext_ref[:, 0:SUBLANES, :] = last8
    return out


def _branch_a_kernel(bcx_ref, g_ref, prev_ref, wc_ref, wout_ref, o_ref, st_ref, ext_ref, *, nseq, lt, valid):
    c_dim = SC_DIM
    rows = nseq * lt
    bcx = bcx_ref[...]
    scb = bcx[:, :c_dim]
    u = (bcx[:, c_dim:2 * c_dim] * bcx[:, 2 * c_dim:]).reshape(nseq, lt, c_dim)
    v = _causal_conv(u, prev_ref, wc_ref[...], st_ref, ext_ref, valid=valid)
    ya = _dot((scb * v.reshape(rows, c_dim)).astype(BF16), wout_ref[...])
    o_ref[...] = _sigmoid(g_ref[...]) * ya


def _branch_a(proj, prev, prev_layer, st_prev, w_conv, w_out, *, layer, nb, ls, nseq, lt, valid):
    nj = ls // lt
    rows = nseq * lt
    n_in = 5
    prevs, pspecs, aliases = _alias_args([st_prev], n_in, 1)
    kern = _skip_alias_refs(functools.partial(_branch_a_kernel, nseq=nseq, lt=lt, valid=valid), n_in, len(prevs))
    return pl.pallas_call(
        kern,
        grid=(nb // nseq, nj),
        in_specs=[pl.BlockSpec((rows, 3 * SC_DIM), lambda i, j: (i * nj + j, COL_BCX // (3 * SC_DIM))),
                  pl.BlockSpec((rows, D_MODEL), lambda i, j: (i * nj + j, COL_G // D_MODEL)),
                  pl.BlockSpec((None, nseq, SC_CONV_W - 1, SC_DIM), lambda i, j: (prev_layer, i, 0, 0)),
                  pl.BlockSpec((None, SC_CONV_W, SC_DIM), lambda i, j: (layer, 0, 0)),
                  pl.BlockSpec((None, SC_DIM, D_MODEL), lambda i, j: (layer, 0, 0))] + pspecs,
        out_specs=[pl.BlockSpec((rows, D_MODEL), lambda i, j: (i * nj + j, 0)),
                   pl.BlockSpec((None, nseq, SC_CONV_W - 1, SC_DIM), lambda i, j: (layer, i, 0, 0))],
        out_shape=[jax.ShapeDtypeStruct((nb * ls, D_MODEL), F32),
                   jax.ShapeDtypeStruct((DEPTH, nb, SC_CONV_W - 1, SC_DIM), F32)],
        scratch_shapes=[pltpu.VMEM((nseq, 2 * SUBLANES, SC_DIM), F32)],
        input_output_aliases=aliases,
        compiler_params=_params("arbitrary", "arbitrary"),
        name="branch_a",
    )(proj, proj, prev, w_conv, w_out, *prevs)


def _load_head(kv_ref, h):
    slabs = [kv_ref[:, :, h, c * LANES:(c + 1) * LANES] for c in range(MEM_HEAD_DIM // LANES)]
    return jnp.concatenate(slabs, axis=-1).astype(BF16)


def _branch_m_kernel(q_ref, g_ref, k_ref, v_ref, wout_ref, o_ref, *, nseq, lt):
    rows = nseq * lt
    scale = MEM_HEAD_DIM ** -0.5
    qb = q_ref[...].astype(BF16).reshape(nseq, lt, D_MODEL)
    outs = []
    for h in range(MEM_HEADS):
        sl = slice(h * MEM_HEAD_DIM, (h + 1) * MEM_HEAD_DIM)
        kh = _load_head(k_ref, h)
        vh = _load_head(v_ref, h)
        s = jnp.einsum("sld,smd->slm", qb[:, :, sl], kh, preferred_element_type=F32) * scale
        s = s - jnp.max(s, axis=-1, keepdims=True)
        p = jnp.exp(s)
        p = p / jnp.sum(p, axis=-1, keepdims=True)
        outs.append(jnp.einsum("slm,smd->sld", p.astype(BF16), vh, preferred_element_type=F32))
    o = jnp.concatenate(outs, axis=-1).reshape(rows, D_MODEL)
    o_ref[...] = _sigmoid(g_ref[...]) * _dot(o.astype(BF16), wout_ref[...])


def _branch_m(proj, mem_k, mem_v, w_out, *, layer, nb, ls, nseq, lt):
    nj = ls // lt
    rows = nseq * lt
    kern = functools.partial(_branch_m_kernel, nseq=nseq, lt=lt)
    kv_spec = pl.BlockSpec((None, nseq, MEM_LEN, MEM_HEADS, MEM_HEAD_DIM), lambda i, j: (layer, i, 0, 0, 0))
    return pl.pallas_call(
        kern,
        grid=(nb // nseq, nj),
        in_specs=[pl.BlockSpec((rows, D_MODEL), lambda i, j: (i * nj + j, COL_Q // D_MODEL)),
                  pl.BlockSpec((rows, D_MODEL), lambda i, j: (i * nj + j, COL_G // D_MODEL + 2)),
                  kv_spec, kv_spec,
                  pl.BlockSpec((None, D_MODEL, D_MODEL), lambda i, j: (layer, 0, 0))],
        out_specs=pl.BlockSpec((rows, D_MODEL), lambda i, j: (i * nj + j, 0)),
        out_shape=jax.ShapeDtypeStruct((nb * ls, D_MODEL), F32),
        compiler_params=_params("parallel", "arbitrary"),
        name="branch_m",
    )(proj, proj, mem_k, mem_v, w_out)


def _softplus(x):
    return jnp.maximum(x, 0.0) + jnp.log1p(jnp.exp(-jnp.abs(x)))


def _ssd_chunk(s, c, xc_ref, z_ref, dt_ref, h_ref, yn_ref, a_row, d_row, nw_row, expand, *, lt, q):
    r0 = pl.multiple_of(c * q, q)
    row0 = pl.multiple_of(s * lt + c * q, q)
    xs = xc_ref[s, pl.ds(r0, q), 0:SSD_INNER]
    bm = xc_ref[s, pl.ds(r0, q), SSD_INNER:SSD_INNER + SSD_BC].astype(BF16)
    cm = xc_ref[s, pl.ds(r0, q), SSD_INNER + SSD_BC:SSD_CONV_DIM].astype(BF16)
    dt = dt_ref[pl.ds(row0, q), :]
    da = dt * a_row
    ti = lax.broadcasted_iota(jnp.int32, (q, q), 0)
    si = lax.broadcasted_iota(jnp.int32, (q, q), 1)
    causal = si <= ti
    tri = jnp.where(causal, 1.0, 0.0).astype(BF16)
    acs = _tri_cumsum(tri, da)
    acs_t = _split_dot(da, tri, 3, dims=(((0,), (1,)), ((), ())))
    acs_last = acs[q - 1:q, :]
    eacs = jnp.exp(acs)
    eacs_hi = eacs.astype(BF16).astype(F32)
    stack = jnp.concatenate([dt, jnp.exp(acs_last - acs), eacs_hi, eacs - eacs_hi], axis=0)
    stack_e = _dot(stack.astype(BF16), expand)
    dt_e, dte_e = stack_e[0:q], stack_e[q:2 * q]
    eacs_e = stack_e[2 * q:3 * q] + stack_e[3 * q:4 * q]
    xdt = xs * dt_e
    xdt_b = xdt.astype(BF16)
    xdte_b = (xdt * dte_e).astype(BF16)
    dec_full = jnp.broadcast_to(jnp.exp(acs_t[:, q - 1:q]), (LANES, SSD_STATE))
    lane = lax.broadcasted_iota(jnp.int32, (q, LANES), 1)
    lo_mask = lane < SSD_HEADDIM
    y_groups = []
    for g in range(SSD_GROUPS):
        gs = slice(g * SSD_STATE, (g + 1) * SSD_STATE)
        gi = slice(g * GROUP_INNER, (g + 1) * GROUP_INNER)
        cb = lax.dot_general(cm[:, gs], bm[:, gs], (((1,), (1,)), ((), ())), preferred_element_type=F32)
        h0 = g * HEADS_PER_GROUP
        hg = h_ref[s, h0:h0 + HEADS_PER_GROUP].reshape(GROUP_INNER, SSD_STATE)
        y_int = lax.dot_general(cm[:, gs], hg.astype(BF16), (((1,), (1,)), ((), ())),
                                preferred_element_type=F32) * eacs_e[:, gi]
        upd = lax.dot_general(xdte_b[:, gi], bm[:, gs], (((0,), (0,)), ((), ())), preferred_element_type=F32)
        y_pairs = []
        for pr in range(HEADS_PER_GROUP // 2):
            col = g * GROUP_INNER + pr * LANES
            xpair = xdt_b[:, col:col + LANES]
            zero = jnp.zeros_like(xpair)
            masks = []
            for hh in range(2):
                h = h0 + 2 * pr + hh
                seg = acs[:, h:h + 1] - acs_t[h:h + 1, :]
                lm = jnp.exp(jnp.where(causal, seg, -jnp.inf))
                masks.append((cb * lm).astype(BF16))
            x_lo = jnp.where(lo_mask, xpair, zero)
            x_hi = jnp.where(lo_mask, zero, xpair)
            if q % LANES == 0:
                y_pairs.append(_dot(jnp.concatenate(masks, axis=1), jnp.concatenate([x_lo, x_hi], axis=0)))
            else:
                y_pairs.append(_dot(masks[0], x_lo) + _dot(masks[1], x_hi))
        y_groups.append(jnp.concatenate(y_pairs, axis=-1) + y_int)
        for hh in range(HEADS_PER_GROUP):
            h = h0 + hh
            h_ref[s, h] = h_ref[s, h] * dec_full[h:h + 1, :] + upd[hh * SSD_HEADDIM:(hh + 1) * SSD_HEADDIM, :]
    y = jnp.concatenate(y_groups, axis=-1) + d_row * xs
    yf = y * _silu(z_ref[pl.ds(row0, q), :])
    normed = []
    for g in range(SSD_GROUPS):
        yg = yf[:, g * GROUP_INNER:(g + 1) * GROUP_INNER]
        normed.append(yg * lax.rsqrt(jnp.mean(yg * yg, axis=-1, keepdims=True) + RMS_EPS))
    yn_ref[pl.ds(row0, q), :] = jnp.concatenate(normed, axis=-1) * nw_row


def _tri_cumsum(tri, da):
    acc = None
    rest = da
    for t in range(3):
        piece = rest.astype(BF16)
        part = _dot(tri, piece)
        acc = part if acc is None else acc + part
        if t < 2:
            rest = rest - piece.astype(F32)
    return acc


def _branch_b_kernel(xbc_ref, z_ref, dtr_ref, g_ref, cprev_ref, h0_ref, wc_ref, bc_ref, dtb_ref, alog_ref,
                     d_ref, nw_ref, exp_ref, wout_ref, o_ref, cst_ref, h_ref, ext_ref, xc_ref, dt_ref, yn_ref,
                     *, nseq, lt, q, valid):
    rows = nseq * lt

    @pl.when(pl.program_id(1) == 0)
    def _():
        h_ref[...] = h0_ref[...]

    conv = _causal_conv(xbc_ref[...].reshape(nseq, lt, SSD_CONV_DIM), cprev_ref, wc_ref[...], cst_ref, ext_ref,
                        valid=valid)
    xc_ref[...] = _silu(conv + bc_ref[...])
    dt = _softplus(dtr_ref[...] + dtb_ref[...])
    if valid < lt:
        t_in_seq = lax.broadcasted_iota(jnp.int32, (nseq, lt, LANES), 1).reshape(rows, LANES)
        dt = jnp.where(t_in_seq < valid, dt, 0.0)
    dt_ref[...] = dt
    a_row = -jnp.exp(alog_ref[...])
    d_row = d_ref[...]
    nw_row = nw_ref[...]
    expand = exp_ref[...]
    nchunk = lt // q

    def body(n, carry):
        _ssd_chunk(n // nchunk, n % nchunk, xc_ref, z_ref, dt_ref, h_ref, yn_ref, a_row, d_row, nw_row, expand,
                   lt=lt, q=q)
        return carry

    lax.fori_loop(0, nseq * nchunk, body, 0)
    o_ref[...] = _sigmoid(g_ref[...]) * _dot(yn_ref[...].astype(BF16), wout_ref[...])


def _branch_b(proj, dtr, cprev, h0, prev_layer, cst_prev, hst_prev, w_conv, b_conv, dt_bias, a_log, d_exp, norm_w,
              expand, w_out, *, layer, nb, ls, nseq, lt, q, valid):
    nj = ls // lt
    rows = nseq * lt
    n_in = 14
    prevs, pspecs, aliases = _alias_args([cst_prev, hst_prev], n_in, 1)
    kern = _skip_alias_refs(functools.partial(_branch_b_kernel, nseq=nseq, lt=lt, q=q, valid=valid),
                            n_in, len(prevs))
    row_blk = lambda i, j: (i * nj + j, 0)
    lay3 = lambda i, j: (layer, 0, 0)
    return pl.pallas_call(
        kern,
        grid=(nb // nseq, nj),
        in_specs=[pl.BlockSpec((rows, SSD_CONV_DIM), lambda i, j: (i * nj + j, COL_XBC // SSD_CONV_DIM)),
                  pl.BlockSpec((rows, SSD_INNER), lambda i, j: (i * nj + j, COL_Z // SSD_INNER)),
                  pl.BlockSpec((rows, LANES), row_blk),
                  pl.BlockSpec((rows, D_MODEL), lambda i, j: (i * nj + j, COL_G // D_MODEL + 1)),
                  pl.BlockSpec((None, nseq, SSD_CONV_W - 1, SSD_CONV_DIM), lambda i, j: (prev_layer, i, 0, 0)),
                  pl.BlockSpec((None, nseq, SSD_HEADS, SSD_HEADDIM, SSD_STATE),
                               lambda i, j: (prev_layer, i, 0, 0, 0)),
                  pl.BlockSpec((None, SSD_CONV_W, SSD_CONV_DIM), lay3),
                  pl.BlockSpec((None, 1, SSD_CONV_DIM), lay3),
                  pl.BlockSpec((None, 1, LANES), lay3),
                  pl.BlockSpec((None, 1, LANES), lay3),
                  pl.BlockSpec((None, 1, SSD_INNER), lay3),
                  pl.BlockSpec((None, 1, SSD_INNER), lay3),
                  pl.BlockSpec((LANES, SSD_INNER), lambda i, j: (0, 0)),
                  pl.BlockSpec((None, SSD_INNER, D_MODEL), lay3)] + pspecs,
        out_specs=[pl.BlockSpec((rows, D_MODEL), row_blk),
                   pl.BlockSpec((None, nseq, SSD_CONV_W - 1, SSD_CONV_DIM), lambda i, j: (layer, i, 0, 0)),
                   pl.BlockSpec((None, nseq, SSD_HEADS, SSD_HEADDIM, SSD_STATE),
                                lambda i, j: (layer, i, 0, 0, 0))],
        out_shape=[jax.ShapeDtypeStruct((nb * ls, D_MODEL), F32),
                   jax.ShapeDtypeStruct((DEPTH, nb, SSD_CONV_W - 1, SSD_CONV_DIM), F32),
                   jax.ShapeDtypeStruct((DEPTH, nb, SSD_HEADS, SSD_HEADDIM, SSD_STATE), F32)],
        scratch_shapes=[pltpu.VMEM((nseq, 2 * SUBLANES, SSD_CONV_DIM), F32),
                        pltpu.VMEM((nseq, lt, SSD_CONV_DIM), F32),
                        pltpu.VMEM((rows, LANES), F32),
                        pltpu.VMEM((rows, SSD_INNER), F32)],
        input_output_aliases=aliases,
        compiler_params=_params("arbitrary", "arbitrary"),
        name="branch_b",
    )(proj, proj, dtr, proj, cprev, h0, w_conv, b_conv, dt_bias, a_log, d_exp, norm_w, expand, w_out, *prevs)


def _merge_kernel(ya_ref, yb_ref, ym_ref, x_ref, wo_ref, g_ref, b_ref, wrh_ref, wrl_ref, br_ref, x1_ref,
                  route_ref):
    merged = (ya_ref[...] + yb_ref[...]) + ym_ref[...]
    mix = _dot(merged.astype(BF16), wo_ref[...])
    x1 = _layer_norm(ALPHA * x_ref[...] + mix, g_ref[...], b_ref[...])
    x1_ref[...] = x1
    x_hi = x1.astype(BF16)
    x_lo = (x1 - x_hi.astype(F32)).astype(BF16)
    w_hi = wrh_ref[...]
    logits = (_dot(x_hi, w_hi) + (_dot(x_lo, w_hi) + _dot(x_hi, wrl_ref[...]))) + br_ref[...]
    lane = lax.broadcasted_iota(jnp.int32, logits.shape, 1).astype(F32)
    neg = -jnp.inf
    big = float(LANES)
    is_grp = lane < N_EXP_GROUPS
    gl = jnp.where(is_grp, logits, neg)
    gmax = jnp.max(gl, axis=-1, keepdims=True)
    gsum = jnp.sum(jnp.exp(gl - gmax), axis=-1, keepdims=True)
    g_idx = jnp.min(jnp.where(jnp.logical_and(is_grp, gl == gmax), lane, big), axis=-1, keepdims=True)
    g_w = 1.0 / gsum
    e_lo = N_EXP_GROUPS + g_idx * EXP_PER_GROUP
    in_grp = jnp.logical_and(lane >= e_lo, lane < e_lo + EXP_PER_GROUP)
    el = jnp.where(in_grp, logits, neg)
    emax = jnp.max(el, axis=-1, keepdims=True)
    ee = jnp.exp(el - emax)
    esum = jnp.sum(ee, axis=-1, keepdims=True)
    i0 = jnp.min(jnp.where(jnp.logical_and(in_grp, el == emax), lane, big), axis=-1, keepdims=True)
    el2 = jnp.where(lane == i0, neg, el)
    emax2 = jnp.max(el2, axis=-1, keepdims=True)
    i1 = jnp.min(jnp.where(jnp.logical_and(in_grp, el2 == emax2), lane, big), axis=-1, keepdims=True)
    w0 = 1.0 / esum
    w1 = jnp.exp(emax2 - emax) / esum
    wsum = w0 + w1
    gate0 = g_w * w0 / wsum
    gate1 = g_w * w1 / wsum
    id0 = i0 - N_EXP_GROUPS
    id1 = i1 - N_EXP_GROUPS
    route = jnp.where(lane == 0, id0, jnp.where(lane == 1, id1, jnp.where(lane == 2, gate0,
                      jnp.where(lane == 3, gate1, 0.0))))
    route_ref[...] = route


def _merge(ya, yb, ym, x, w_o, ln_g, ln_b, w_r_hi, w_r_lo, b_r, *, layer, tm):
    t = x.shape[0]
    row = lambda i: (i, 0)
    lay = lambda i: (layer, 0, 0)
    return pl.pallas_call(
        _merge_kernel,
        grid=(t // tm,),
        in_specs=[pl.BlockSpec((tm, D_MODEL), row)] * 4 + [
            pl.BlockSpec((None, D_MODEL, D_MODEL), lay),
            pl.BlockSpec((None, 1, D_MODEL), lay), pl.BlockSpec((None, 1, D_MODEL), lay),
            pl.BlockSpec((None, D_MODEL, LANES), lay), pl.BlockSpec((None, D_MODEL, LANES), lay),
            pl.BlockSpec((None, 1, LANES), lay)],
        out_specs=[pl.BlockSpec((tm, D_MODEL), row), pl.BlockSpec((tm, LANES), row)],
        out_shape=[jax.ShapeDtypeStruct((t, D_MODEL), F32), jax.ShapeDtypeStruct((t, LANES), F32)],
        compiler_params=_params("parallel"),
        name="merge_ln_router",
    )(ya, yb, ym, x, w_o, ln_g, ln_b, w_r_hi, w_r_lo, b_r)


def _moe_kernel(bexp_ref, nused_ref, starts_ref, pstarts_ref, counts_ref, tok_ref, row_ref, gate_ref, x_ref,
                win_ref, wout_ref, g_ref, b_ref, o_ref, xg_ref, eo_ref, y2_ref, *, nblk, tc):
    c = pl.program_id(0)
    i = pl.program_id(1)

    @pl.when(i < nused_ref[c])
    def _():
        tbl = c * N_EXPERTS + bexp_ref[c * nblk + i]
        off0 = i * MOE_BLOCK - pstarts_ref[tbl]
        pos0 = starts_ref[tbl] + off0
        nvalid = counts_ref[tbl] - off0

        for r in range(MOE_BLOCK):
            tok = tok_ref[0, 0, pos0 + r]
            xg_ref[r:r + 1, :] = x_ref[pl.ds(tok, 1), :]
        gu = _dot(xg_ref[...].astype(BF16), win_ref[...])
        hid = _silu(gu[:, :EXP_FF]) * gu[:, EXP_FF:]
        eo_ref[...] = _dot(hid.astype(BF16), wout_ref[...])
        for r in range(MOE_BLOCK):
            row = jnp.where(r < nvalid, row_ref[0, 0, pos0 + r], EXP_TOPK * tc)
            y2_ref[pl.ds(row, 1), :] = gate_ref[0, 0, pos0 + r] * eo_ref[r:r + 1, :]

    @pl.when(i == nblk - 1)
    def _():
        ffn = y2_ref[0:tc, :] + y2_ref[tc:2 * tc, :]
        o_ref[...] = _layer_norm(ALPHA * x_ref[...] + ffn, g_ref[...], b_ref[...])


def _moe(x1, tables, slot_arrays, w_in, w_out, ln_g, ln_b, *, layer, tc, nblk):
    t = x1.shape[0]
    nchunks = t // tc
    s = tc * EXP_TOPK
    kern = functools.partial(_moe_kernel, nblk=nblk, tc=tc)
    smem_row = lambda c, i, *_: (c, 0, 0)
    lay = lambda c, i, *_: (layer, 0, 0)
    expert = lambda c, i, be, *_: (layer, be[c * nblk + i], 0, 0)
    grid_spec = pltpu.PrefetchScalarGridSpec(
        num_scalar_prefetch=5,
        grid=(nchunks, nblk),
        in_specs=[pl.BlockSpec((1, 1, s + MOE_BLOCK), smem_row, memory_space=pltpu.SMEM)] * 3 + [
                  pl.BlockSpec((tc, D_MODEL), lambda c, i, *_: (c, 0), pipeline_mode=pl.Buffered(1)),
                  pl.BlockSpec((None, None, D_MODEL, 2 * EXP_FF), expert),
                  pl.BlockSpec((None, None, EXP_FF, D_MODEL), expert),
                  pl.BlockSpec((None, 1, D_MODEL), lay),
                  pl.BlockSpec((None, 1, D_MODEL), lay)],
        out_specs=pl.BlockSpec((tc, D_MODEL), lambda c, i, *_: (c, 0)),
        scratch_shapes=[pltpu.VMEM((MOE_BLOCK, D_MODEL), F32), pltpu.VMEM((MOE_BLOCK, D_MODEL), F32),
                        pltpu.VMEM((EXP_TOPK * tc + SUBLANES, D_MODEL), F32)],
    )
    return pl.pallas_call(
        kern,
        grid_spec=grid_spec,
        out_shape=jax.ShapeDtypeStruct((t, D_MODEL), F32),
        compiler_params=_params("arbitrary", "arbitrary"),
        name="moe_experts",
    )(*tables, *[a[:, None, :] for a in slot_arrays], x1, w_in, w_out, ln_g, ln_b)


def _dispatch(route, *, tc, nblk):
    t = route.shape[0]
    nchunks = t // tc
    s = tc * EXP_TOPK
    ids = route[:, 0:EXP_TOPK].astype(jnp.int32).reshape(nchunks, s)
    gates = route[:, EXP_TOPK:2 * EXP_TOPK].reshape(nchunks, s)
    slot = jnp.broadcast_to(jnp.arange(s, dtype=jnp.int32)[None, :], (nchunks, s))
    _, order, gate_sorted = lax.sort((ids, slot, gates), dimension=1, is_stable=True, num_keys=1)
    tok_sorted = order // EXP_TOPK
    row_sorted = (order % EXP_TOPK) * tc + tok_sorted
    counts =jnp.sum((ids[:, :, None] == jnp.arange(N_EXPERTS, dtype=jnp.int32)).astype(jnp.int32), axis=1)
    starts = jnp.cumsum(counts, axis=-1) - counts
    padded = (counts + MOE_BLOCK - 1) // MOE_BLOCK * MOE_BLOCK
    pends = jnp.cumsum(padded, axis=-1)
    pstarts = pends - padded
    blk_start = jnp.arange(nblk, dtype=jnp.int32) * MOE_BLOCK
    bexp = jnp.sum((pends[:, None, :] <= blk_start[None, :, None]).astype(jnp.int32), axis=-1)
    bexp = jnp.minimum(bexp, N_EXPERTS - 1).reshape(-1)
    nused = pends[:, -1] // MOE_BLOCK
    tables = (bexp, nused, starts.reshape(-1), pstarts.reshape(-1), counts.reshape(-1))
    pad = ((0, 0), (0, MOE_BLOCK))
    return tables, (jnp.pad(tok_sorted, pad), jnp.pad(row_sorted, pad), jnp.pad(gate_sorted, pad))


def _prep_weights(w_in, w_sconv_out, w_ssd_out, w_mem_k, w_mem_v, w_mem_out, w_o, w_exp_in, w_exp_out,
                  w_router_grp, b_router_grp, w_router_exp, b_router_exp, ssd_dt_bias, ssd_a_log, ssd_d):
    o_b, o_z, o_xbc, o_dt, o_q, o_g = 0, 3072, 5120, 8192, 8224, 9248
    w_main = jnp.concatenate([w_in[:, :, o_xbc:o_dt], w_in[:, :, o_b:o_z], w_in[:, :, o_z:o_xbc],
                              w_in[:, :, o_q:o_g], w_in[:, :, o_g:]], axis=-1).astype(BF16)
    pad_h = LANES - SSD_HEADS
    w_dt = jnp.pad(w_in[:, :, o_dt:o_q], ((0, 0), (0, 0), (0, pad_h))).astype(BF16)
    pad_r = LANES - N_EXP_GROUPS - N_EXPERTS
    w_r = jnp.pad(jnp.concatenate([w_router_grp, w_router_exp], axis=-1), ((0, 0), (0, 0), (0, pad_r)))
    w_r_hi = w_r.astype(BF16)
    w_r_lo = (w_r - w_r_hi.astype(F32)).astype(BF16)
    b_r = jnp.pad(jnp.concatenate([b_router_grp, b_router_exp], axis=-1), ((0, 0), (0, pad_r)))[:, None, :]
    head_of_col = jnp.arange(SSD_INNER, dtype=jnp.int32) // SSD_HEADDIM
    expand = (jnp.arange(LANES, dtype=jnp.int32)[:, None] == head_of_col[None, :]).astype(BF16)
    return dict(
        w_main=w_main, w_dt=w_dt, w_r_hi=w_r_hi, w_r_lo=w_r_lo, b_r=b_r, expand=expand,
        w_sconv_out=w_sconv_out.astype(BF16), w_ssd_out=w_ssd_out.astype(BF16),
        w_mem_kv=jnp.concatenate([w_mem_k, w_mem_v], axis=-1).astype(BF16),
        w_mem_out=w_mem_out.astype(BF16), w_o=w_o.astype(BF16),
        w_exp_in=w_exp_in.astype(BF16), w_exp_out=w_exp_out.astype(BF16),
        dt_bias=jnp.pad(ssd_dt_bias, ((0, 0), (0, pad_h)))[:, None, :],
        a_log=jnp.pad(ssd_a_log, ((0, 0), (0, pad_h)))[:, None, :],
        d_exp=jnp.repeat(ssd_d, SSD_HEADDIM, axis=-1)[:, None, :],
    )


def _moe_blocks(tc):
    return (tc * EXP_TOPK + N_EXPERTS * (MOE_BLOCK - 1) + MOE_BLOCK - 1) // MOE_BLOCK


def _decoder_layer(l, x, mem_k, mem_v, states_in, prev_layer, states_out, wp, p, cfg):
    nb, ls = cfg["nb"], cfg["ls"]
    sc_in, cv_in, h_in = states_in
    sc_out, cv_out, h_out = states_out
    proj = _matmul(x, wp["w_main"], l, cfg["tm_proj"], 2048)
    dtr = _matmul(x, wp["w_dt"], l, cfg["tm_proj"], LANES)
    ya, sc_new = _branch_a(proj, sc_in, prev_layer, sc_out, p["w_sconv"], wp["w_sconv_out"],
                           layer=l, nb=nb, ls=ls, nseq=cfg["a_nseq"], lt=cfg["a_lt"], valid=cfg["a_valid"])
    yb, cv_new, h_new = _branch_b(proj, dtr, cv_in, h_in, prev_layer, cv_out, h_out, p["w_ssd_conv"],
                                  p["b_ssd_conv"], wp["dt_bias"], wp["a_log"], wp["d_exp"], p["ssd_norm_w"],
                                  wp["expand"], wp["w_ssd_out"],
                                  layer=l, nb=nb, ls=ls, nseq=cfg["b_nseq"], lt=cfg["b_lt"], q=cfg["b_q"],
                                  valid=cfg["b_valid"])
    ym = _branch_m(proj, mem_k, mem_v, wp["w_mem_out"], layer=l, nb=nb, ls=ls, nseq=cfg["m_nseq"],
                   lt=cfg["m_lt"])
    x1, route = _merge(ya, yb, ym, x, wp["w_o"], p["ln1_g"], p["ln1_b"], wp["w_r_hi"], wp["w_r_lo"], wp["b_r"],
                       layer=l, tm=cfg["tm_merge"])
    tc = cfg["moe_tc"]
    nblk = _moe_blocks(tc)
    tables, slot_arrays = _dispatch(route, tc=tc, nblk=nblk)
    x2 = _moe(x1, tables, slot_arrays, wp["w_exp_in"], wp["w_exp_out"], p["ln2_g"], p["ln2_b"],
              layer=l, tc=tc, nblk=nblk)
    return x2, (sc_new, cv_new, h_new)


PROMPT_CFG = dict(tm_proj=1024, a_nseq=1, a_lt=512, b_nseq=1, b_lt=256, b_q=SSD_CHUNK, m_nseq=1, m_lt=512,
                  tm_merge=512, moe_tc=2048)
SAMPLE_PAD = SUBLANES


def _zero_states(nb):
    return (jnp.zeros((DEPTH, nb, SC_CONV_W - 1, SC_DIM), F32),
            jnp.zeros((DEPTH, nb, SSD_CONV_W - 1, SSD_CONV_DIM), F32),
            jnp.zeros((DEPTH, nb, SSD_HEADS, SSD_HEADDIM, SSD_STATE), F32))


def kernel(x_prompt, x_sample, mem_prompt, cache_mem_k, cache_mem_v, state_sconv, state_ssd_conv, state_ssd, w_in, w_sconv, w_sconv_out, w_ssd_conv, b_ssd_conv, ssd_dt_bias, ssd_a_log, ssd_d, ssd_norm_w, w_ssd_out, w_mem_k, w_mem_v, w_mem_out, w_o, ln1_g, ln1_b, w_router_grp, b_router_grp, w_router_exp, b_router_exp, w_exp_in, w_exp_out, ln2_g, ln2_b):
    bp, seq, _ = x_prompt.shape
    bs, dec_seq, _ = x_sample.shape
    wp = _prep_weights(w_in, w_sconv_out, w_ssd_out, w_mem_k, w_mem_v, w_mem_out, w_o, w_exp_in, w_exp_out,
                       w_router_grp, b_router_grp, w_router_exp, b_router_exp, ssd_dt_bias, ssd_a_log, ssd_d)
    p = dict(w_sconv=w_sconv, w_ssd_conv=w_ssd_conv, b_ssd_conv=b_ssd_conv[:, None, :],
             ssd_norm_w=ssd_norm_w[:, None, :], ln1_g=ln1_g[:, None, :], ln1_b=ln1_b[:, None, :],
             ln2_g=ln2_g[:, None, :], ln2_b=ln2_b[:, None, :])

    cfg_p = dict(PROMPT_CFG, nb=bp, ls=seq, a_valid=PROMPT_CFG["a_lt"], b_valid=PROMPT_CFG["b_lt"])
    mem_rows = mem_prompt.reshape(bp * MEM_LEN, D_MODEL)
    zeros_in = (jnp.zeros((1, bp, SC_CONV_W - 1, SC_DIM), F32),
                jnp.zeros((1, bp, SSD_CONV_W - 1, SSD_CONV_DIM), F32),
                jnp.zeros((1, bp, SSD_HEADS, SSD_HEADDIM, SSD_STATE), F32))
    yp = x_prompt.reshape(bp * seq, D_MODEL)
    mk_p = jnp.zeros((DEPTH, bp, MEM_LEN, MEM_HEADS, MEM_HEAD_DIM), F32)
    mv_p = jnp.zeros((DEPTH, bp, MEM_LEN, MEM_HEADS, MEM_HEAD_DIM), F32)
    st_p = _zero_states(bp)
    for l in range(DEPTH):
        mk_p = _memkv(mem_rows, wp["w_mem_kv"], l, 0, mk_p, nb=bp, nbt=4)
        mv_p = _memkv(mem_rows, wp["w_mem_kv"], l, 1, mv_p, nb=bp, nbt=4)
        yp, st_p = _decoder_layer(l, yp, mk_p, mv_p, zeros_in, 0, st_p, wp, p, cfg_p)

    ls = SAMPLE_PAD
    cfg_s = dict(tm_proj=bs * ls, nb=bs, ls=ls, a_nseq=16, a_lt=ls, a_valid=dec_seq, b_nseq=4, b_lt=ls, b_q=ls,
                 b_valid=dec_seq, m_nseq=2, m_lt=ls, tm_merge=512, moe_tc=bs * ls)
    ys = jnp.pad(x_sample, ((0, 0), (0, ls - dec_seq), (0, 0))).reshape(bs * ls, D_MODEL)
    st_s = _zero_states(bs)
    for l in range(DEPTH):
        ys, st_s = _decoder_layer(l, ys, cache_mem_k, cache_mem_v, (state_sconv, state_ssd_conv, state_ssd), l,
                                  st_s, wp, p, cfg_s)

    y_prompt = yp.reshape(bp, seq, D_MODEL)
    y_sample = ys.reshape(bs, ls, D_MODEL)[:, :dec_seq]
    return (y_prompt, y_sample, mk_p, mv_p, st_p[0], st_p[1], st_p[2], st_s[0], st_s[1], st_s[2])
```

```python
import functools

import jax
import jax.numpy as jnp
from jax import lax
from jax.experimental import pallas as pl
from jax.experimental.pallas import tpu as pltpu

F32 = jnp.float32
BF16 = jnp.bfloat16

D_MODEL = 1024
DEPTH = 4
SC_DIM = D_MODEL
SC_CONV_W = 3
SSD_INNER = 2 * D_MODEL
SSD_HEADDIM = 64
SSD_HEADS = SSD_INNER // SSD_HEADDIM
SSD_GROUPS = 4
SSD_STATE = 128
SSD_CONV_W = 4
SSD_BC = SSD_GROUPS * SSD_STATE
SSD_CONV_DIM = SSD_INNER + 2 * SSD_BC
SSD_CHUNK = 128
HEADS_PER_GROUP = SSD_HEADS // SSD_GROUPS
GROUP_INNER = SSD_INNER // SSD_GROUPS
MEM_LEN = 256
MEM_HEADS = 4
MEM_HEAD_DIM = D_MODEL // MEM_HEADS
N_EXP_GROUPS = 4
EXP_PER_GROUP = 8
N_EXPERTS = N_EXP_GROUPS * EXP_PER_GROUP
EXP_TOPK = 2
EXP_FF = 512
MOE_BLOCK = 128
MOE_FINISH_ROWS = 256
ALPHA = (2 * DEPTH) ** 0.25
LN_EPS = 1e-5
RMS_EPS = 1e-6

LANES = 128
SUBLANES = 8
VMEM_LIMIT = 56 * 1024 * 1024

COL_XBC = 0
COL_BCX = SSD_CONV_DIM
COL_Z = COL_BCX + 3 * SC_DIM
COL_Q = COL_Z + SSD_INNER
COL_G = COL_Q + D_MODEL
PROJ_COLS = COL_G + 3 * D_MODEL


def _dot(a, b):
    return jnp.dot(a, b, preferred_element_type=F32)


def _split_dot(a, b, terms, dims=(((1,), (0,)), ((), ()))):
    acc = None
    rest = a
    for t in range(terms):
        piece = rest.astype(BF16)
        part = lax.dot_general(piece, b, dims, preferred_element_type=F32)
        acc = part if acc is None else acc + part
        if t + 1 < terms:
            rest = rest - piece.astype(F32)
    return acc


def _sigmoid(x):
    return 1.0 / (1.0 + jnp.exp(-x))


def _silu(x):
    return x * _sigmoid(x)


def _layer_norm(x, g, b):
    mu = jnp.mean(x, axis=-1, keepdims=True)
    xc = x - mu
    var = jnp.mean(xc * xc, axis=-1, keepdims=True)
    return xc * lax.rsqrt(var + LN_EPS) * g + b


def _params(*sem):
    return pltpu.CompilerParams(dimension_semantics=sem, vmem_limit_bytes=VMEM_LIMIT)


def _skip_alias_refs(kernel_fn, n_in, n_alias):
    def wrapped(*refs):
        return kernel_fn(*refs[:n_in], *refs[n_in + n_alias:])
    return wrapped


def _alias_args(prevs, n_in, first_out):
    prevs = [p for p in prevs if p is not None]
    specs = [pl.BlockSpec(memory_space=pl.ANY)] * len(prevs)
    aliases = {n_in + k: first_out + k for k in range(len(prevs))}
    return prevs, specs, aliases


def _mm_kernel(x_ref, w_ref, o_ref):
    o_ref[...] = _dot(x_ref[...].astype(BF16), w_ref[...])


def _matmul(x, w, layer, tm, tn):
    m, k = x.shape
    n = w.shape[2]
    tm = min(tm, m)
    tn = min(tn, n)
    return pl.pallas_call(
        _mm_kernel,
        grid=(m // tm, n // tn),
        in_specs=[pl.BlockSpec((tm, k), lambda i, j: (i, 0)),
                  pl.BlockSpec((None, k, tn), lambda i, j: (layer, 0, j))],
        out_specs=pl.BlockSpec((tm, tn), lambda i, j: (i, j)),
        out_shape=jax.ShapeDtypeStruct((m, n), F32),
        compiler_params=_params("parallel", "arbitrary"),
        name="matmul",
    )(x, w)


def _memkv_kernel(x_ref, w_ref, o_ref, *, nbt):
    res = _dot(x_ref[...].astype(BF16), w_ref[...])
    for h in range(MEM_HEADS):
        o_ref[:, :, h, :] = res[:, h * MEM_HEAD_DIM:(h + 1) * MEM_HEAD_DIM].reshape(nbt, MEM_LEN, MEM_HEAD_DIM)


def _memkv(mem_rows, w_kv, layer, which, stacked_prev, *, nb, nbt):
    prevs, pspecs, aliases = _alias_args([stacked_prev], 2, 0)
    kern = _skip_alias_refs(functools.partial(_memkv_kernel, nbt=nbt), 2, len(prevs))
    return pl.pallas_call(
        kern,
        grid=(nb // nbt,),
        in_specs=[pl.BlockSpec((nbt * MEM_LEN, D_MODEL), lambda i: (i, 0)),
                  pl.BlockSpec((None, D_MODEL, D_MODEL), lambda i: (layer, 0, which))] + pspecs,
        out_specs=pl.BlockSpec((None, nbt, MEM_LEN, MEM_HEADS, MEM_HEAD_DIM), lambda i: (layer, i, 0, 0, 0)),
        out_shape=jax.ShapeDtypeStruct((DEPTH, nb, MEM_LEN, MEM_HEADS, MEM_HEAD_DIM), F32),
        input_output_aliases=aliases,
        compiler_params=_params("parallel"),
        name="mem_kv",
    )(mem_rows, w_kv, *prevs)


def _shift_rows(u, tail, d):
    nseq, lt, c_dim = u.shape
    groups = lt // SUBLANES
    row = lax.broadcasted_iota(jnp.int32, (SUBLANES, c_dim), 0)
    outs = []
    for s in range(nseq):
        rot = pltpu.roll(u[s].reshape(groups, SUBLANES, c_dim), d, axis=1)
        before = pltpu.roll(tail[s], d, axis=0)[None]
        if groups > 1:
            before = jnp.concatenate([before, rot[:groups - 1]], axis=0)
        outs.append(jnp.where(row < d, before, rot).reshape(lt, c_dim))
    return jnp.stack(outs, axis=0)


def _causal_conv(u, prev_ref, w, st_ref, ext_ref, *, valid):
    nseq, lt, _ = u.shape
    width = w.shape[0]
    halo = width - 1

    @pl.when(pl.program_id(1) == 0)
    def _():
        ext_ref[:, 0:SUBLANES, :] = jnp.zeros_like(ext_ref[:, 0:SUBLANES, :])
        ext_ref[:, SUBLANES - halo:SUBLANES, :] = prev_ref[...]

    tail = ext_ref[:, 0:SUBLANES, :]
    last8 = u[:, lt - SUBLANES:lt, :]
    ext_ref[:, SUBLANES:2 * SUBLANES, :] = last8
    out = _shift_rows(u, tail, halo) * w[0:1, :]
    for k in range(1, halo):
        out = out + _shift_rows(u, tail, halo - k) * w[k:k + 1, :]
    out = out + u * w[halo:width, :]
    end = 2 * SUBLANES - (lt - valid)
    st_ref[...] = ext_ref[:, end - halo:end, :]
    ext_ref[:, 0:SUBLANES, :] = last8
    return out


def _branch_a_kernel(bcx_ref, g_ref, prev_ref, wc_ref, wout_ref, o_ref, st_ref, ext_ref, *, nseq, lt, valid):
    c_dim = SC_DIM
    rows = nseq * lt
    bcx = bcx_ref[...]
    scb = bcx[:, :c_dim]
    u = (bcx[:, c_dim:2 * c_dim] * bcx[:, 2 * c_dim:]).reshape(nseq, lt, c_dim)
    v = _causal_conv(u, prev_ref, wc_ref[...], st_ref, ext_ref, valid=valid)
    ya = _dot((scb * v.reshape(rows, c_dim)).astype(BF16), wout_ref[...])
    o_ref[...] = _sigmoid(g_ref[...]) * ya


def _branch_a(proj, prev, prev_layer, st_prev, w_conv, w_out, *, layer, nb, ls, nseq, lt, valid):
    nj = ls // lt
    rows = nseq * lt
    n_in = 5
    prevs, pspecs, aliases = _alias_args([st_prev], n_in, 1)
    kern = _skip_alias_refs(functools.partial(_branch_a_kernel, nseq=nseq, lt=lt, valid=valid), n_in, len(prevs))
    return pl.pallas_call(
        kern,
        grid=(nb // nseq, nj),
        in_specs=[pl.BlockSpec((rows, 3 * SC_DIM), lambda i, j: (i * nj + j, COL_BCX // (3 * SC_DIM))),
                  pl.BlockSpec((rows, D_MODEL), lambda i, j: (i * nj + j, COL_G // D_MODEL)),
                  pl.BlockSpec((None, nseq, SC_CONV_W - 1, SC_DIM), lambda i, j: (prev_layer, i, 0, 0)),
                  pl.BlockSpec((None, SC_CONV_W, SC_DIM), lambda i, j: (layer, 0, 0)),
                  pl.BlockSpec((None, SC_DIM, D_MODEL), lambda i, j: (layer, 0, 0))] + pspecs,
        out_specs=[pl.BlockSpec((rows, D_MODEL), lambda i, j: (i * nj + j, 0)),
                   pl.BlockSpec((None, nseq, SC_CONV_W - 1, SC_DIM), lambda i, j: (layer, i, 0, 0))],
        out_shape=[jax.ShapeDtypeStruct((nb * ls, D_MODEL), F32),
                   jax.ShapeDtypeStruct((DEPTH, nb, SC_CONV_W - 1, SC_DIM), F32)],
        scratch_shapes=[pltpu.VMEM((nseq, 2 * SUBLANES, SC_DIM), F32)],
        input_output_aliases=aliases,
        compiler_params=_params("arbitrary", "arbitrary"),
        name="branch_a",
    )(proj, proj, prev, w_conv, w_out, *prevs)


def _load_head(kv_ref, h):
    slabs = [kv_ref[:, :, h, c * LANES:(c + 1) * LANES] for c in range(MEM_HEAD_DIM // LANES)]
    return jnp.concatenate(slabs, axis=-1).astype(BF16)


def _branch_m_kernel(q_ref, g_ref, k_ref, v_ref, wout_ref, o_ref, *, nseq, lt):
    rows = nseq * lt
    scale = MEM_HEAD_DIM ** -0.5
    qb = q_ref[...].astype(BF16).reshape(nseq, lt, D_MODEL)
    outs = []
    for h in range(MEM_HEADS):
        sl = slice(h * MEM_HEAD_DIM, (h + 1) * MEM_HEAD_DIM)
        kh = _load_head(k_ref, h)
        vh = _load_head(v_ref, h)
        s = jnp.einsum("sld,smd->slm", qb[:, :, sl], kh, preferred_element_type=F32) * scale
        s = s - jnp.max(s, axis=-1, keepdims=True)
        p = jnp.exp(s)
        p = p / jnp.sum(p, axis=-1, keepdims=True)
        outs.append(jnp.einsum("slm,smd->sld", p.astype(BF16), vh, preferred_element_type=F32))
    o = jnp.concatenate(outs, axis=-1).reshape(rows, D_MODEL)
    o_ref[...] = _sigmoid(g_ref[...]) * _dot(o.astype(BF16), wout_ref[...])


def _branch_m(proj, mem_k, mem_v, w_out, *, layer, nb, ls, nseq, lt):
    nj = ls // lt
    rows = nseq * lt
    kern = functools.partial(_branch_m_kernel, nseq=nseq, lt=lt)
    kv_spec = pl.BlockSpec((None, nseq, MEM_LEN, MEM_HEADS, MEM_HEAD_DIM), lambda i, j: (layer, i, 0, 0, 0))
    return pl.pallas_call(
        kern,
        grid=(nb // nseq, nj),
        in_specs=[pl.BlockSpec((rows, D_MODEL), lambda i, j: (i * nj + j, COL_Q // D_MODEL)),
                  pl.BlockSpec((rows, D_MODEL), lambda i, j: (i * nj + j, COL_G // D_MODEL + 2)),
                  kv_spec, kv_spec,
                  pl.BlockSpec((None, D_MODEL, D_MODEL), lambda i, j: (layer, 0, 0))],
        out_specs=pl.BlockSpec((rows, D_MODEL), lambda i, j: (i * nj + j, 0)),
        out_shape=jax.ShapeDtypeStruct((nb * ls, D_MODEL), F32),
        compiler_params=_params("parallel", "arbitrary"),
        name="branch_m",
    )(proj, proj, mem_k, mem_v, w_out)


def _softplus(x):
    return jnp.maximum(x, 0.0) + jnp.log1p(jnp.exp(-jnp.abs(x)))


def _ssd_chunk(s, c, xc_ref, z_ref, dt_ref, h_ref, yn_ref, a_row, d_row, nw_row, expand, *, lt, q):
    r0 = pl.multiple_of(c * q, q)
    row0 = pl.multiple_of(s * lt + c * q, q)
    xs = xc_ref[s, pl.ds(r0, q), 0:SSD_INNER]
    bm = xc_ref[s, pl.ds(r0, q), SSD_INNER:SSD_INNER + SSD_BC].astype(BF16)
    cm = xc_ref[s, pl.ds(r0, q), SSD_INNER + SSD_BC:SSD_CONV_DIM].astype(BF16)
    dt = dt_ref[pl.ds(row0, q), :]
    da = dt * a_row
    ti = lax.broadcasted_iota(jnp.int32, (q, q), 0)
    si = lax.broadcasted_iota(jnp.int32, (q, q), 1)
    causal = si <= ti
    tri = jnp.where(causal, 1.0, 0.0).astype(BF16)
    acs = _tri_cumsum(tri, da)
    acs_t = _split_dot(da, tri, 3, dims=(((0,), (1,)), ((), ())))
    acs_last = acs[q - 1:q, :]
    eacs = jnp.exp(acs)
    eacs_hi = eacs.astype(BF16).astype(F32)
    stack = jnp.concatenate([dt, jnp.exp(acs_last - acs), eacs_hi, eacs - eacs_hi], axis=0)
    stack_e = _dot(stack.astype(BF16), expand)
    dt_e, dte_e = stack_e[0:q], stack_e[q:2 * q]
    eacs_e = stack_e[2 * q:3 * q] + stack_e[3 * q:4 * q]
    xdt = xs * dt_e
    xdt_b = xdt.astype(BF16)
    xdte_b = (xdt * dte_e).astype(BF16)
    dec_full = jnp.broadcast_to(jnp.exp(acs_t[:, q - 1:q]), (LANES, SSD_STATE))
    lane = lax.broadcasted_iota(jnp.int32, (q, LANES), 1)
    lo_mask = lane < SSD_HEADDIM
    y_groups = []
    for g in range(SSD_GROUPS):
        gs = slice(g * SSD_STATE, (g + 1) * SSD_STATE)
        gi = slice(g * GROUP_INNER, (g + 1) * GROUP_INNER)
        cb = lax.dot_general(cm[:, gs], bm[:, gs], (((1,), (1,)), ((), ())), preferred_element_type=F32)
        h0 = g * HEADS_PER_GROUP
        hg = h_ref[s, h0:h0 + HEADS_PER_GROUP].reshape(GROUP_INNER, SSD_STATE)
        y_int = lax.dot_general(cm[:, gs], hg.astype(BF16), (((1,), (1,)), ((), ())),
                                preferred_element_type=F32) * eacs_e[:, gi]
        upd = lax.dot_general(xdte_b[:, gi], bm[:, gs], (((0,), (0,)), ((), ())), preferred_element_type=F32)
        y_pairs = []
        for pr in range(HEADS_PER_GROUP // 2):
            col = g * GROUP_INNER + pr * LANES
            xpair = xdt_b[:, col:col + LANES]
            zero = jnp.zeros_like(xpair)
            masks = []
            for hh in range(2):
                h = h0 + 2 * pr + hh
                seg = acs[:, h:h + 1] - acs_t[h:h + 1, :]
                lm = jnp.exp(jnp.where(causal, seg, -jnp.inf))
                masks.append((cb * lm).astype(BF16))
            x_lo = jnp.where(lo_mask, xpair, zero)
            x_hi = jnp.where(lo_mask, zero, xpair)
            if q % LANES == 0:
                y_pairs.append(_dot(jnp.concatenate(masks, axis=1), jnp.concatenate([x_lo, x_hi], axis=0)))
            else:
                y_pairs.append(_dot(masks[0], x_lo) + _dot(masks[1], x_hi))
        y_groups.append(jnp.concatenate(y_pairs, axis=-1) + y_int)
        for hh in range(HEADS_PER_GROUP):
            h = h0 + hh
            h_ref[s, h] = h_ref[s, h] * dec_full[h:h + 1, :] + upd[hh * SSD_HEADDIM:(hh + 1) * SSD_HEADDIM, :]
    y = jnp.concatenate(y_groups, axis=-1) + d_row * xs
    yf = y * _silu(z_ref[pl.ds(row0, q), :])
    normed = []
    for g in range(SSD_GROUPS):
        yg = yf[:, g * GROUP_INNER:(g + 1) * GROUP_INNER]
        normed.append(yg * lax.rsqrt(jnp.mean(yg * yg, axis=-1, keepdims=True) + RMS_EPS))
    yn_ref[pl.ds(row0, q), :] = jnp.concatenate(normed, axis=-1) * nw_row


def _tri_cumsum(tri, da):
    acc = None
    rest = da
    for t in range(3):
        piece = rest.astype(BF16)
        part = _dot(tri, piece)
        acc = part if acc is None else acc + part
        if t < 2:
            rest = rest - piece.astype(F32)
    return acc


def _branch_b_kernel(xbc_ref, z_ref, dtr_ref, g_ref, cprev_ref, h0_ref, wc_ref, bc_ref, dtb_ref, alog_ref,
                     d_ref, nw_ref, exp_ref, wout_ref, o_ref, cst_ref, h_ref, ext_ref, xc_ref, dt_ref, yn_ref,
                     *, nseq, lt, q, valid):
    rows = nseq * lt

    @pl.when(pl.program_id(1) == 0)
    def _():
        h_ref[...] = h0_ref[...]

    conv = _causal_conv(xbc_ref[...].reshape(nseq, lt, SSD_CONV_DIM), cprev_ref, wc_ref[...], cst_ref, ext_ref,
                        valid=valid)
    xc_ref[...] = _silu(conv + bc_ref[...])
    dt = _softplus(dtr_ref[...] + dtb_ref[...])
    if valid < lt:
        t_in_seq = lax.broadcasted_iota(jnp.int32, (nseq, lt, LANES), 1).reshape(rows, LANES)
        dt = jnp.where(t_in_seq < valid, dt, 0.0)
    dt_ref[...] = dt
    a_row = -jnp.exp(alog_ref[...])
    d_row = d_ref[...]
    nw_row = nw_ref[...]
    expand = exp_ref[...]
    nchunk = lt // q

    def body(n, carry):
        _ssd_chunk(n // nchunk, n % nchunk, xc_ref, z_ref, dt_ref, h_ref, yn_ref, a_row, d_row, nw_row, expand,
                   lt=lt, q=q)
        return carry

    lax.fori_loop(0, nseq * nchunk, body, 0)
    o_ref[...] = _sigmoid(g_ref[...]) * _dot(yn_ref[...].astype(BF16), wout_ref[...])


def _branch_b(proj, dtr, cprev, h0, prev_layer, cst_prev, hst_prev, w_conv, b_conv, dt_bias, a_log, d_exp, norm_w,
              expand, w_out, *, layer, nb, ls, nseq, lt, q, valid):
    nj = ls // lt
    rows = nseq * lt
    n_in = 14
    prevs, pspecs, aliases = _alias_args([cst_prev, hst_prev], n_in, 1)
    kern = _skip_alias_refs(functools.partial(_branch_b_kernel, nseq=nseq, lt=lt, q=q, valid=valid),
                            n_in, len(prevs))
    row_blk = lambda i, j: (i * nj + j, 0)
    lay3 = lambda i, j: (layer, 0, 0)
    return pl.pallas_call(
        kern,
        grid=(nb // nseq, nj),
        in_specs=[pl.BlockSpec((rows, SSD_CONV_DIM), lambda i, j: (i * nj + j, COL_XBC // SSD_CONV_DIM)),
                  pl.BlockSpec((rows, SSD_INNER), lambda i, j: (i * nj + j, COL_Z // SSD_INNER)),
                  pl.BlockSpec((rows, LANES), row_blk),
                  pl.BlockSpec((rows, D_MODEL), lambda i, j: (i * nj + j, COL_G // D_MODEL + 1)),
                  pl.BlockSpec((None, nseq, SSD_CONV_W - 1, SSD_CONV_DIM), lambda i, j: (prev_layer, i, 0, 0)),
                  pl.BlockSpec((None, nseq, SSD_HEADS, SSD_HEADDIM, SSD_STATE),
                               lambda i, j: (prev_layer, i, 0, 0, 0)),
                  pl.BlockSpec((None, SSD_CONV_W, SSD_CONV_DIM), lay3),
                  pl.BlockSpec((None, 1, SSD_CONV_DIM), lay3),
                  pl.BlockSpec((None, 1, LANES), lay3),
                  pl.BlockSpec((None, 1, LANES), lay3),
                  pl.BlockSpec((None, 1, SSD_INNER), lay3),
                  pl.BlockSpec((None, 1, SSD_INNER), lay3),
                  pl.BlockSpec((LANES, SSD_INNER), lambda i, j: (0, 0)),
                  pl.BlockSpec((None, SSD_INNER, D_MODEL), lay3)] + pspecs,
        out_specs=[pl.BlockSpec((rows, D_MODEL), row_blk),
                   pl.BlockSpec((None, nseq, SSD_CONV_W - 1, SSD_CONV_DIM), lambda i, j: (layer, i, 0, 0)),
                   pl.BlockSpec((None, nseq, SSD_HEADS, SSD_HEADDIM, SSD_STATE),
                                lambda i, j: (layer, i, 0, 0, 0))],
        out_shape=[jax.ShapeDtypeStruct((nb * ls, D_MODEL), F32),
                   jax.ShapeDtypeStruct((DEPTH, nb, SSD_CONV_W - 1, SSD_CONV_DIM), F32),
                   jax.ShapeDtypeStruct((DEPTH, nb, SSD_HEADS, SSD_HEADDIM, SSD_STATE), F32)],
        scratch_shapes=[pltpu.VMEM((nseq, 2 * SUBLANES, SSD_CONV_DIM), F32),
                        pltpu.VMEM((nseq, lt, SSD_CONV_DIM), F32),
                        pltpu.VMEM((rows, LANES), F32),
                        pltpu.VMEM((rows, SSD_INNER), F32)],
        input_output_aliases=aliases,
        compiler_params=_params("arbitrary", "arbitrary"),
        name="branch_b",
    )(proj, proj, dtr, proj, cprev, h0, w_conv, b_conv, dt_bias, a_log, d_exp, norm_w, expand, w_out, *prevs)


def _merge_kernel(ya_ref, yb_ref, ym_ref, x_ref, wo_ref, g_ref, b_ref, wrh_ref, wrl_ref, br_ref, x1_ref,
                  route_ref):
    merged = (ya_ref[...] + yb_ref[...]) + ym_ref[...]
    mix = _dot(merged.astype(BF16), wo_ref[...])
    x1 = _layer_norm(ALPHA * x_ref[...] + mix, g_ref[...], b_ref[...])
    nslab = D_MODEL // LANES
    for j in range(nslab):
        x1_ref[pl.ds(j, x1.shape[0], stride=nslab), :] = x1[:, j * LANES:(j + 1) * LANES]
    x_hi = x1.astype(BF16)
    x_lo = (x1 - x_hi.astype(F32)).astype(BF16)
    w_hi = wrh_ref[...]
    logits = (_dot(x_hi, w_hi) + (_dot(x_lo, w_hi) + _dot(x_hi, wrl_ref[...]))) + br_ref[...]
    lane = lax.broadcasted_iota(jnp.int32, logits.shape, 1).astype(F32)
    neg = -jnp.inf
    big = float(LANES)
    is_grp = lane < N_EXP_GROUPS
    gl = jnp.where(is_grp, logits, neg)
    gmax = jnp.max(gl, axis=-1, keepdims=True)
    gsum = jnp.sum(jnp.exp(gl - gmax), axis=-1, keepdims=True)
    g_idx = jnp.min(jnp.where(jnp.logical_and(is_grp, gl == gmax), lane, big), axis=-1, keepdims=True)
    g_w = 1.0 / gsum
    e_lo = N_EXP_GROUPS + g_idx * EXP_PER_GROUP
    in_grp = jnp.logical_and(lane >= e_lo, lane < e_lo + EXP_PER_GROUP)
    el = jnp.where(in_grp, logits, neg)
    emax = jnp.max(el, axis=-1, keepdims=True)
    ee = jnp.exp(el - emax)
    esum = jnp.sum(ee, axis=-1, keepdims=True)
    i0 = jnp.min(jnp.where(jnp.logical_and(in_grp, el == emax), lane, big), axis=-1, keepdims=True)
    el2 = jnp.where(lane == i0, neg, el)
    emax2 = jnp.max(el2, axis=-1, keepdims=True)
    i1 = jnp.min(jnp.where(jnp.logical_and(in_grp, el2 == emax2), lane, big), axis=-1, keepdims=True)
    w0 = 1.0 / esum
    w1 = jnp.exp(emax2 - emax) / esum
    wsum = w0 + w1
    gate0 = g_w * w0 / wsum
    gate1 = g_w * w1 / wsum
    id0 = i0 - N_EXP_GROUPS
    id1 = i1 - N_EXP_GROUPS
    route = jnp.where(lane == 0, id0, jnp.where(lane == 1, id1, jnp.where(lane == 2, gate0,
                      jnp.where(lane == 3, gate1, 0.0))))
    route_ref[...] = route


def _merge(ya, yb, ym, x, w_o, ln_g, ln_b, w_r_hi, w_r_lo, b_r, *, layer, tm):
    t = x.shape[0]
    row = lambda i: (i, 0)
    lay = lambda i: (layer, 0, 0)
    return pl.pallas_call(
        _merge_kernel,
        grid=(t // tm,),
        in_specs=[pl.BlockSpec((tm, D_MODEL), row)] * 4 + [
            pl.BlockSpec((None, D_MODEL, D_MODEL), lay),
            pl.BlockSpec((None, 1, D_MODEL), lay), pl.BlockSpec((None, 1, D_MODEL), lay),
            pl.BlockSpec((None, D_MODEL, LANES), lay), pl.BlockSpec((None, D_MODEL, LANES), lay),
            pl.BlockSpec((None, 1, LANES), lay)],
        out_specs=[pl.BlockSpec((tm * (D_MODEL // LANES), LANES), row), pl.BlockSpec((tm, LANES), row)],
        out_shape=[jax.ShapeDtypeStruct((t * (D_MODEL // LANES), LANES), F32),
                   jax.ShapeDtypeStruct((t, LANES), F32)],
        compiler_params=_params("parallel"),
        name="merge_ln_router",
    )(ya, yb, ym, x, w_o, ln_g, ln_b, w_r_hi, w_r_lo, b_r)


def _moe_kernel(bexp_ref, nused_ref, starts_ref, pstarts_ref, counts_ref, tok_ref, row_ref, gate_ref, x_ref,
                win_ref, wout_ref, g_ref, b_ref, o_ref, xg_ref, eo_ref, y2_ref, *, nblk, tc):
    c = pl.program_id(0)
    i = pl.program_id(1)
    nslab = D_MODEL // LANES

    def tile(t):
        return pl.ds(pl.multiple_of(t * nslab, nslab), nslab)

    @pl.when(i < nused_ref[c])
    def _():
        tbl = c * N_EXPERTS + bexp_ref[c * nblk + i]
        off0 = i * MOE_BLOCK - pstarts_ref[tbl]
        pos0 = starts_ref[tbl] + off0
        nvalid = counts_ref[tbl] - off0

        for r in range(MOE_BLOCK):
            xg_ref[r * nslab:(r + 1) * nslab, :] = x_ref[tile(tok_ref[0, 0, pos0 + r]), :]
        gu = _dot(_token_rows(xg_ref, 0, MOE_BLOCK).astype(BF16), win_ref[...])
        hid = _silu(gu[:, :EXP_FF]) * gu[:, EXP_FF:]
        eo = _dot(hid.astype(BF16), wout_ref[...])
        for j in range(nslab):
            eo_ref[pl.ds(j, MOE_BLOCK, stride=nslab), :] = eo[:, j * LANES:(j + 1) * LANES]
        for r in range(MOE_BLOCK):
            row = jnp.where(r < nvalid, row_ref[0, 0, pos0 + r], EXP_TOPK * tc)
            y2_ref[tile(row), :] = gate_ref[0, 0, pos0 + r] * eo_ref[r * nslab:(r + 1) * nslab, :]

    @pl.when(i == nblk - 1)
    def _():
        step = min(MOE_FINISH_ROWS, tc)

        def finish(blk, carry):
            t0 = pl.multiple_of(blk * step, step)
            ffn = _token_rows(y2_ref, t0, step) + _token_rows(y2_ref, tc + t0, step)
            o_ref[pl.ds(t0, step), :] = _layer_norm(ALPHA * _token_rows(x_ref, t0, step) + ffn,
                                                    g_ref[...], b_ref[...])
            return carry

        lax.fori_loop(0, tc // step, finish, 0)


def _token_rows(ref, t0, n):
    nslab = D_MODEL // LANES
    return jnp.concatenate([ref[pl.ds(t0 * nslab + j, n, stride=nslab), :] for j in range(nslab)], axis=-1)


def _moe(x1_tiles, tables, slot_arrays, w_in, w_out, ln_g, ln_b, *, layer, tc, nblk):
    nslab = D_MODEL // LANES
    t = x1_tiles.shape[0] // nslab
    nchunks = t // tc
    s = tc * EXP_TOPK
    kern = functools.partial(_moe_kernel, nblk=nblk, tc=tc)
    smem_row = lambda c, i, *_: (c, 0, 0)
    lay = lambda c, i, *_: (layer, 0, 0)
    expert = lambda c, i, be, *_: (layer, be[c * nblk + i], 0, 0)
    grid_spec = pltpu.PrefetchScalarGridSpec(
        num_scalar_prefetch=5,
        grid=(nchunks, nblk),
        in_specs=[pl.BlockSpec((1, 1, s + MOE_BLOCK), smem_row, memory_space=pltpu.SMEM)] * 3 + [
                  pl.BlockSpec((tc * nslab, LANES), lambda c, i, *_: (c, 0), pipeline_mode=pl.Buffered(1)),
                  pl.BlockSpec((None, None, D_MODEL, 2 * EXP_FF), expert),
                  pl.BlockSpec((None, None, EXP_FF, D_MODEL), expert),
                  pl.BlockSpec((None, 1, D_MODEL), lay),
                  pl.BlockSpec((None, 1, D_MODEL), lay)],
        out_specs=pl.BlockSpec((tc, D_MODEL), lambda c, i, *_: (c, 0)),
        scratch_shapes=[pltpu.VMEM((MOE_BLOCK * nslab, LANES), F32), pltpu.VMEM((MOE_BLOCK * nslab, LANES), F32),
                        pltpu.VMEM(((EXP_TOPK * tc + 1) * nslab, LANES), F32)],
    )
    return pl.pallas_call(
        kern,
        grid_spec=grid_spec,
        out_shape=jax.ShapeDtypeStruct((t, D_MODEL), F32),
        compiler_params=_params("arbitrary", "arbitrary"),
        name="moe_experts",
    )(*tables, *[a[:, None, :] for a in slot_arrays], x1_tiles, w_in, w_out, ln_g, ln_b)


def _dispatch(route, *, tc, nblk):
    t = route.shape[0]
    nchunks = t // tc
    s = tc * EXP_TOPK
    ids = route[:, 0:EXP_TOPK].astype(jnp.int32).reshape(nchunks, s)
    gates = route[:, EXP_TOPK:2 * EXP_TOPK].reshape(nchunks, s)
    slot = jnp.broadcast_to(jnp.arange(s, dtype=jnp.int32)[None, :], (nchunks, s))
    _, order, gate_sorted = lax.sort((ids, slot, gates), dimension=1, is_stable=True, num_keys=1)
    tok_sorted = order // EXP_TOPK
    row_sorted = (order % EXP_TOPK) * tc + tok_sorted
    counts =jnp.sum((ids[:, :, None] == jnp.arange(N_EXPERTS, dtype=jnp.int32)).astype(jnp.int32), axis=1)
    starts = jnp.cumsum(counts, axis=-1) - counts
    padded = (counts + MOE_BLOCK - 1) // MOE_BLOCK * MOE_BLOCK
    pends = jnp.cumsum(padded, axis=-1)
    pstarts = pends - padded
    blk_start = jnp.arange(nblk, dtype=jnp.int32) * MOE_BLOCK
    bexp = jnp.sum((pends[:, None, :] <= blk_start[None, :, None]).astype(jnp.int32), axis=-1)
    bexp = jnp.minimum(bexp, N_EXPERTS - 1).reshape(-1)
    nused = pends[:, -1] // MOE_BLOCK
    tables = (bexp, nused, starts.reshape(-1), pstarts.reshape(-1), counts.reshape(-1))
    pad = ((0, 0), (0, MOE_BLOCK))
    return tables, (jnp.pad(tok_sorted, pad), jnp.pad(row_sorted, pad), jnp.pad(gate_sorted, pad))


def _prep_weights(w_in, w_sconv_out, w_ssd_out, w_mem_k, w_mem_v, w_mem_out, w_o, w_exp_in, w_exp_out,
                  w_router_grp, b_router_grp, w_router_exp, b_router_exp, ssd_dt_bias, ssd_a_log, ssd_d):
    o_b, o_z, o_xbc, o_dt, o_q, o_g = 0, 3072, 5120, 8192, 8224, 9248
    w_main = jnp.concatenate([w_in[:, :, o_xbc:o_dt], w_in[:, :, o_b:o_z], w_in[:, :, o_z:o_xbc],
                              w_in[:, :, o_q:o_g], w_in[:, :, o_g:]], axis=-1).astype(BF16)
    pad_h = LANES - SSD_HEADS
    w_dt = jnp.pad(w_in[:, :, o_dt:o_q], ((0, 0), (0, 0), (0, pad_h))).astype(BF16)
    pad_r = LANES - N_EXP_GROUPS - N_EXPERTS
    w_r = jnp.pad(jnp.concatenate([w_router_grp, w_router_exp], axis=-1), ((0, 0), (0, 0), (0, pad_r)))
    w_r_hi = w_r.astype(BF16)
    w_r_lo = (w_r - w_r_hi.astype(F32)).astype(BF16)
    b_r = jnp.pad(jnp.concatenate([b_router_grp, b_router_exp], axis=-1), ((0, 0), (0, pad_r)))[:, None, :]
    head_of_col = jnp.arange(SSD_INNER, dtype=jnp.int32) // SSD_HEADDIM
    expand = (jnp.arange(LANES, dtype=jnp.int32)[:, None] == head_of_col[None, :]).astype(BF16)
    return dict(
        w_main=w_main, w_dt=w_dt, w_r_hi=w_r_hi, w_r_lo=w_r_lo, b_r=b_r, expand=expand,
        w_sconv_out=w_sconv_out.astype(BF16), w_ssd_out=w_ssd_out.astype(BF16),
        w_mem_kv=jnp.concatenate([w_mem_k, w_mem_v], axis=-1).astype(BF16),
        w_mem_out=w_mem_out.astype(BF16), w_o=w_o.astype(BF16),
        w_exp_in=w_exp_in.astype(BF16), w_exp_out=w_exp_out.astype(BF16),
        dt_bias=jnp.pad(ssd_dt_bias, ((0, 0), (0, pad_h)))[:, None, :],
        a_log=jnp.pad(ssd_a_log, ((0, 0), (0, pad_h)))[:, None, :],
        d_exp=jnp.repeat(ssd_d, SSD_HEADDIM, axis=-1)[:, None, :],
    )


def _moe_blocks(tc):
    return (tc * EXP_TOPK + N_EXPERTS * (MOE_BLOCK - 1) + MOE_BLOCK - 1) // MOE_BLOCK


def _decoder_layer(l, x, mem_k, mem_v, states_in, prev_layer, states_out, wp, p, cfg):
    nb, ls = cfg["nb"], cfg["ls"]
    sc_in, cv_in, h_in = states_in
    sc_out, cv_out, h_out = states_out
    proj = _matmul(x, wp["w_main"], l, cfg["tm_proj"], 2048)
    dtr = _matmul(x, wp["w_dt"], l, cfg["tm_proj"], LANES)
    ya, sc_new = _branch_a(proj, sc_in, prev_layer, sc_out, p["w_sconv"], wp["w_sconv_out"],
                           layer=l, nb=nb, ls=ls, nseq=cfg["a_nseq"], lt=cfg["a_lt"], valid=cfg["a_valid"])
    yb, cv_new, h_new = _branch_b(proj, dtr, cv_in, h_in, prev_layer, cv_out, h_out, p["w_ssd_conv"],
                                  p["b_ssd_conv"], wp["dt_bias"], wp["a_log"], wp["d_exp"], p["ssd_norm_w"],
                                  wp["expand"], wp["w_ssd_out"],
                                  layer=l, nb=nb, ls=ls, nseq=cfg["b_nseq"], lt=cfg["b_lt"], q=cfg["b_q"],
                                  valid=cfg["b_valid"])
    ym = _branch_m(proj, mem_k, mem_v, wp["w_mem_out"], layer=l, nb=nb, ls=ls, nseq=cfg["m_nseq"],
                   lt=cfg["m_lt"])
    x1, route = _merge(ya, yb, ym, x, wp["w_o"], p["ln1_g"], p["ln1_b"], wp["w_r_hi"], wp["w_r_lo"], wp["b_r"],
                       layer=l, tm=cfg["tm_merge"])
    tc = cfg["moe_tc"]
    nblk = _moe_blocks(tc)
    tables, slot_arrays = _dispatch(route, tc=tc, nblk=nblk)
    x2 = _moe(x1, tables, slot_arrays, wp["w_exp_in"], wp["w_exp_out"], p["ln2_g"], p["ln2_b"],
              layer=l, tc=tc, nblk=nblk)
    return x2, (sc_new, cv_new, h_new)


PROMPT_CFG = dict(tm_proj=1024, a_nseq=1, a_lt=512, b_nseq=1, b_lt=256, b_q=SSD_CHUNK, m_nseq=1, m_lt=512,
                  tm_merge=512, moe_tc=2048)
SAMPLE_PAD = SUBLANES


def _zero_states(nb):
    return (jnp.zeros((DEPTH, nb, SC_CONV_W - 1, SC_DIM), F32),
            jnp.zeros((DEPTH, nb, SSD_CONV_W - 1, SSD_CONV_DIM), F32),
            jnp.zeros((DEPTH, nb, SSD_HEADS, SSD_HEADDIM, SSD_STATE), F32))


def kernel(x_prompt, x_sample, mem_prompt, cache_mem_k, cache_mem_v, state_sconv, state_ssd_conv, state_ssd, w_in, w_sconv, w_sconv_out, w_ssd_conv, b_ssd_conv, ssd_dt_bias, ssd_a_log, ssd_d, ssd_norm_w, w_ssd_out, w_mem_k, w_mem_v, w_mem_out, w_o, ln1_g, ln1_b, w_router_grp, b_router_grp, w_router_exp, b_router_exp, w_exp_in, w_exp_out, ln2_g, ln2_b):
    bp, seq, _ = x_prompt.shape
    bs, dec_seq, _ = x_sample.shape
    wp = _prep_weights(w_in, w_sconv_out, w_ssd_out, w_mem_k, w_mem_v, w_mem_out, w_o, w_exp_in, w_exp_out,
                       w_router_grp, b_router_grp, w_router_exp, b_router_exp, ssd_dt_bias, ssd_a_log, ssd_d)
    p = dict(w_sconv=w_sconv, w_ssd_conv=w_ssd_conv, b_ssd_conv=b_ssd_conv[:, None, :],
             ssd_norm_w=ssd_norm_w[:, None, :], ln1_g=ln1_g[:, None, :], ln1_b=ln1_b[:, None, :],
             ln2_g=ln2_g[:, None, :], ln2_b=ln2_b[:, None, :])

    cfg_p = dict(PROMPT_CFG, nb=bp, ls=seq, a_valid=PROMPT_CFG["a_lt"], b_valid=PROMPT_CFG["b_lt"])
    mem_rows = mem_prompt.reshape(bp * MEM_LEN, D_MODEL)
    zeros_in = (jnp.zeros((1, bp, SC_CONV_W - 1, SC_DIM), F32),
                jnp.zeros((1, bp, SSD_CONV_W - 1, SSD_CONV_DIM), F32),
                jnp.zeros((1, bp, SSD_HEADS, SSD_HEADDIM, SSD_STATE), F32))
    yp = x_prompt.reshape(bp * seq, D_MODEL)
    mk_p = jnp.zeros((DEPTH, bp, MEM_LEN, MEM_HEADS, MEM_HEAD_DIM), F32)
    mv_p = jnp.zeros((DEPTH, bp, MEM_LEN, MEM_HEADS, MEM_HEAD_DIM), F32)
    st_p = _zero_states(bp)
    for l in range(DEPTH):
        mk_p = _memkv(mem_rows, wp["w_mem_kv"], l, 0, mk_p, nb=bp, nbt=4)
        mv_p = _memkv(mem_rows, wp["w_mem_kv"], l, 1, mv_p, nb=bp, nbt=4)
        yp, st_p = _decoder_layer(l, yp, mk_p, mv_p, zeros_in, 0, st_p, wp, p, cfg_p)

    ls = SAMPLE_PAD
    cfg_s = dict(tm_proj=bs * ls, nb=bs, ls=ls, a_nseq=16, a_lt=ls, a_valid=dec_seq, b_nseq=4, b_lt=ls, b_q=ls,
                 b_valid=dec_seq, m_nseq=2, m_lt=ls, tm_merge=512, moe_tc=bs * ls)
    ys = jnp.pad(x_sample, ((0, 0), (0, ls - dec_seq), (0, 0))).reshape(bs * ls, D_MODEL)
    st_s = _zero_states(bs)
    for l in range(DEPTH):
        ys, st_s = _decoder_layer(l, ys, cache_mem_k, cache_mem_v, (state_sconv, state_ssd_conv, state_ssd), l,
                                  st_s, wp, p, cfg_s)

    y_prompt = yp.reshape(bp, seq, D_MODEL)
    y_sample = ys.reshape(bs, ls, D_MODEL)[:, :dec_seq]
    return (y_prompt, y_sample, mk_p, mv_p, st_p[0], st_p[1], st_p[2], st_s[0], st_s[1], st_s[2])
```

```python
import functools

import jax
import jax.numpy as jnp
from jax import lax
from jax.experimental import pallas as pl
from jax.experimental.pallas import tpu as pltpu

F32 = jnp.float32
BF16 = jnp.bfloat16

D_MODEL = 1024
DEPTH = 4
SC_DIM = D_MODEL
SC_CONV_W = 3
SSD_INNER = 2 * D_MODEL
SSD_HEADDIM = 64
SSD_HEADS = SSD_INNER // SSD_HEADDIM
SSD_GROUPS = 4
SSD_STATE = 128
SSD_CONV_W = 4
SSD_BC = SSD_GROUPS * SSD_STATE
SSD_CONV_DIM = SSD_INNER + 2 * SSD_BC
SSD_CHUNK = 128
HEADS_PER_GROUP = SSD_HEADS // SSD_GROUPS
GROUP_INNER = SSD_INNER // SSD_GROUPS
MEM_LEN = 256
MEM_HEADS = 4
MEM_HEAD_DIM = D_MODEL // MEM_HEADS
N_EXP_GROUPS = 4
EXP_PER_GROUP = 8
N_EXPERTS = N_EXP_GROUPS * EXP_PER_GROUP
EXP_TOPK = 2
EXP_FF = 512
MOE_BLOCK = 128
MOE_FINISH_ROWS = 256
MOE_WEIGHT_SLOTS = 3
ALPHA = (2 * DEPTH) ** 0.25
LN_EPS = 1e-5
RMS_EPS = 1e-6

LANES = 128
SUBLANES = 8
VMEM_LIMIT = 56 * 1024 * 1024

COL_XBC = 0
COL_BCX = SSD_CONV_DIM
COL_Z = COL_BCX + 3 * SC_DIM
COL_Q = COL_Z + SSD_INNER
COL_G = COL_Q + D_MODEL
PROJ_COLS = COL_G + 3 * D_MODEL


def _dot(a, b):
    return jnp.dot(a, b, preferred_element_type=F32)


def _split_dot(a, b, terms, dims=(((1,), (0,)), ((), ()))):
    acc = None
    rest = a
    for t in range(terms):
        piece = rest.astype(BF16)
        part = lax.dot_general(piece, b, dims, preferred_element_type=F32)
        acc = part if acc is None else acc + part
        if t + 1 < terms:
            rest = rest - piece.astype(F32)
    return acc


def _sigmoid(x):
    return 1.0 / (1.0 + jnp.exp(-x))


def _silu(x):
    return x * _sigmoid(x)


def _layer_norm(x, g, b):
    mu = jnp.mean(x, axis=-1, keepdims=True)
    xc = x - mu
    var = jnp.mean(xc * xc, axis=-1, keepdims=True)
    return xc * lax.rsqrt(var + LN_EPS) * g + b


def _params(*sem):
    return pltpu.CompilerParams(dimension_semantics=sem, vmem_limit_bytes=VMEM_LIMIT)


def _skip_alias_refs(kernel_fn, n_in, n_alias):
    def wrapped(*refs):
        return kernel_fn(*refs[:n_in], *refs[n_in + n_alias:])
    return wrapped


def _alias_args(prevs, n_in, first_out):
    prevs = [p for p in prevs if p is not None]
    specs = [pl.BlockSpec(memory_space=pl.ANY)] * len(prevs)
    aliases = {n_in + k: first_out + k for k in range(len(prevs))}
    return prevs, specs, aliases


def _mm_kernel(x_ref, w_ref, o_ref):
    o_ref[...] = _dot(x_ref[...].astype(BF16), w_ref[...])


def _matmul(x, w, layer, tm, tn):
    m, k = x.shape
    n = w.shape[2]
    tm = min(tm, m)
    tn = min(tn, n)
    return pl.pallas_call(
        _mm_kernel,
        grid=(m // tm, n // tn),
        in_specs=[pl.BlockSpec((tm, k), lambda i, j: (i, 0)),
                  pl.BlockSpec((None, k, tn), lambda i, j: (layer, 0, j))],
        out_specs=pl.BlockSpec((tm, tn), lambda i, j: (i, j)),
        out_shape=jax.ShapeDtypeStruct((m, n), F32),
        compiler_params=_params("parallel", "arbitrary"),
        name="matmul",
    )(x, w)


def _memkv_kernel(x_ref, w_ref, o_ref, *, nbt):
    res = _dot(x_ref[...].astype(BF16), w_ref[...])
    for h in range(MEM_HEADS):
        o_ref[:, :, h, :] = res[:, h * MEM_HEAD_DIM:(h + 1) * MEM_HEAD_DIM].reshape(nbt, MEM_LEN, MEM_HEAD_DIM)


def _memkv(mem_rows, w_kv, layer, which, stacked_prev, *, nb, nbt):
    prevs, pspecs, aliases = _alias_args([stacked_prev], 2, 0)
    kern = _skip_alias_refs(functools.partial(_memkv_kernel, nbt=nbt), 2, len(prevs))
    return pl.pallas_call(
        kern,
        grid=(nb // nbt,),
        in_specs=[pl.BlockSpec((nbt * MEM_LEN, D_MODEL), lambda i: (i, 0)),
                  pl.BlockSpec((None, D_MODEL, D_MODEL), lambda i: (layer, 0, which))] + pspecs,
        out_specs=pl.BlockSpec((None, nbt, MEM_LEN, MEM_HEADS, MEM_HEAD_DIM), lambda i: (layer, i, 0, 0, 0)),
        out_shape=jax.ShapeDtypeStruct((DEPTH, nb, MEM_LEN, MEM_HEADS, MEM_HEAD_DIM), F32),
        input_output_aliases=aliases,
        compiler_params=_params("parallel"),
        name="mem_kv",
    )(mem_rows, w_kv, *prevs)


def _shift_rows(u, tail, d):
    nseq, lt, c_dim = u.shape
    groups = lt // SUBLANES
    row = lax.broadcasted_iota(jnp.int32, (SUBLANES, c_dim), 0)
    outs = []
    for s in range(nseq):
        rot = pltpu.roll(u[s].reshape(groups, SUBLANES, c_dim), d, axis=1)
        before = pltpu.roll(tail[s], d, axis=0)[None]
        if groups > 1:
            before = jnp.concatenate([before, rot[:groups - 1]], axis=0)
        outs.append(jnp.where(row < d, before, rot).reshape(lt, c_dim))
    return jnp.stack(outs, axis=0)


def _causal_conv(u, prev_ref, w, st_ref, ext_ref, *, valid):
    nseq, lt, _ = u.shape
    width = w.shape[0]
    halo = width - 1

    @pl.when(pl.program_id(1) == 0)
    def _():
        ext_ref[:, 0:SUBLANES, :] = jnp.zeros_like(ext_ref[:, 0:SUBLANES, :])
        ext_ref[:, SUBLANES - halo:SUBLANES, :] = prev_ref[...]

    tail = ext_ref[:, 0:SUBLANES, :]
    last8 = u[:, lt - SUBLANES:lt, :]
    ext_ref[:, SUBLANES:2 * SUBLANES, :] = last8
    out = _shift_rows(u, tail, halo) * w[0:1, :]
    for k in range(1, halo):
        out = out + _shift_rows(u, tail, halo - k) * w[k:k + 1, :]
    out = out + u * w[halo:width, :]
    end = 2 * SUBLANES - (lt - valid)
    st_ref[...] = ext_ref[:, end - halo:end, :]
    ext_ref[:, 0:SUBLANES, :] = last8
    return out


def _branch_a_kernel(bcx_ref, g_ref, prev_ref, wc_ref, wout_ref, o_ref, st_ref, ext_ref, *, nseq, lt, valid):
    c_dim = SC_DIM
    rows = nseq * lt
    bcx = bcx_ref[...]
    scb = bcx[:, :c_dim]
    u = (bcx[:, c_dim:2 * c_dim] * bcx[:, 2 * c_dim:]).reshape(nseq, lt, c_dim)
    v = _causal_conv(u, prev_ref, wc_ref[...], st_ref, ext_ref, valid=valid)
    ya = _dot((scb * v.reshape(rows, c_dim)).astype(BF16), wout_ref[...])
    o_ref[...] = _sigmoid(g_ref[...]) * ya


def _branch_a(proj, prev, prev_layer, st_prev, w_conv, w_out, *, layer, nb, ls, nseq, lt, valid):
    nj = ls // lt
    rows = nseq * lt
    n_in = 5
    prevs, pspecs, aliases = _alias_args([st_prev], n_in, 1)
    kern = _skip_alias_refs(functools.partial(_branch_a_kernel, nseq=nseq, lt=lt, valid=valid), n_in, len(prevs))
    return pl.pallas_call(
        kern,
        grid=(nb // nseq, nj),
        in_specs=[pl.BlockSpec((rows, 3 * SC_DIM), lambda i, j: (i * nj + j, COL_BCX // (3 * SC_DIM))),
                  pl.BlockSpec((rows, D_MODEL), lambda i, j: (i * nj + j, COL_G // D_MODEL)),
                  pl.BlockSpec((None, nseq, SC_CONV_W - 1, SC_DIM), lambda i, j: (prev_layer, i, 0, 0)),
                  pl.BlockSpec((None, SC_CONV_W, SC_DIM), lambda i, j: (layer, 0, 0)),
                  pl.BlockSpec((None, SC_DIM, D_MODEL), lambda i, j: (layer, 0, 0))] + pspecs,
        out_specs=[pl.BlockSpec((rows, D_MODEL), lambda i, j: (i * nj + j, 0)),
                   pl.BlockSpec((None, nseq, SC_CONV_W - 1, SC_DIM), lambda i, j: (layer, i, 0, 0))],
        out_shape=[jax.ShapeDtypeStruct((nb * ls, D_MODEL), F32),
                   jax.ShapeDtypeStruct((DEPTH, nb, SC_CONV_W - 1, SC_DIM), F32)],
        scratch_shapes=[pltpu.VMEM((nseq, 2 * SUBLANES, SC_DIM), F32)],
        input_output_aliases=aliases,
        compiler_params=_params("arbitrary", "arbitrary"),
        name="branch_a",
    )(proj, proj, prev, w_conv, w_out, *prevs)


def _load_head(kv_ref, h):
    slabs = [kv_ref[:, :, h, c * LANES:(c + 1) * LANES] for c in range(MEM_HEAD_DIM // LANES)]
    return jnp.concatenate(slabs, axis=-1).astype(BF16)


def _branch_m_kernel(q_ref, g_ref, k_ref, v_ref, wout_ref, o_ref, *, nseq, lt):
    rows = nseq * lt
    scale = MEM_HEAD_DIM ** -0.5
    qb = q_ref[...].astype(BF16).reshape(nseq, lt, D_MODEL)
    outs = []
    for h in range(MEM_HEADS):
        sl = slice(h * MEM_HEAD_DIM, (h + 1) * MEM_HEAD_DIM)
        kh = _load_head(k_ref, h)
        vh = _load_head(v_ref, h)
        s = jnp.einsum("sld,smd->slm", qb[:, :, sl], kh, preferred_element_type=F32) * scale
        s = s - jnp.max(s, axis=-1, keepdims=True)
        p = jnp.exp(s)
        p = p / jnp.sum(p, axis=-1, keepdims=True)
        outs.append(jnp.einsum("slm,smd->sld", p.astype(BF16), vh, preferred_element_type=F32))
    o = jnp.concatenate(outs, axis=-1).reshape(rows, D_MODEL)
    o_ref[...] = _sigmoid(g_ref[...]) * _dot(o.astype(BF16), wout_ref[...])


def _branch_m(proj, mem_k, mem_v, w_out, *, layer, nb, ls, nseq, lt):
    nj = ls // lt
    rows = nseq * lt
    kern = functools.partial(_branch_m_kernel, nseq=nseq, lt=lt)
    kv_spec = pl.BlockSpec((None, nseq, MEM_LEN, MEM_HEADS, MEM_HEAD_DIM), lambda i, j: (layer, i, 0, 0, 0))
    return pl.pallas_call(
        kern,
        grid=(nb // nseq, nj),
        in_specs=[pl.BlockSpec((rows, D_MODEL), lambda i, j: (i * nj + j, COL_Q // D_MODEL)),
                  pl.BlockSpec((rows, D_MODEL), lambda i, j: (i * nj + j, COL_G // D_MODEL + 2)),
                  kv_spec, kv_spec,
                  pl.BlockSpec((None, D_MODEL, D_MODEL), lambda i, j: (layer, 0, 0))],
        out_specs=pl.BlockSpec((rows, D_MODEL), lambda i, j: (i * nj + j, 0)),
        out_shape=jax.ShapeDtypeStruct((nb * ls, D_MODEL), F32),
        compiler_params=_params("parallel", "arbitrary"),
        name="branch_m",
    )(proj, proj, mem_k, mem_v, w_out)


def _softplus(x):
    return jnp.maximum(x, 0.0) + jnp.log1p(jnp.exp(-jnp.abs(x)))


def _ssd_chunk(s, c, xc_ref, z_ref, dt_ref, h_ref, yn_ref, a_row, d_row, nw_row, expand, *, lt, q):
    r0 = pl.multiple_of(c * q, q)
    row0 = pl.multiple_of(s * lt + c * q, q)
    xs = xc_ref[s, pl.ds(r0, q), 0:SSD_INNER]
    bm = xc_ref[s, pl.ds(r0, q), SSD_INNER:SSD_INNER + SSD_BC].astype(BF16)
    cm = xc_ref[s, pl.ds(r0, q), SSD_INNER + SSD_BC:SSD_CONV_DIM].astype(BF16)
    dt = dt_ref[pl.ds(row0, q), :]
    da = dt * a_row
    ti = lax.broadcasted_iota(jnp.int32, (q, q), 0)
    si = lax.broadcasted_iota(jnp.int32, (q, q), 1)
    causal = si <= ti
    tri = jnp.where(causal, 1.0, 0.0).astype(BF16)
    acs = _tri_cumsum(tri, da)
    acs_t = _split_dot(da, tri, 3, dims=(((0,), (1,)), ((), ())))
    acs_last = acs[q - 1:q, :]
    eacs = jnp.exp(acs)
    eacs_hi = eacs.astype(BF16).astype(F32)
    stack = jnp.concatenate([dt, jnp.exp(acs_last - acs), eacs_hi, eacs - eacs_hi], axis=0)
    stack_e = _dot(stack.astype(BF16), expand)
    dt_e, dte_e = stack_e[0:q], stack_e[q:2 * q]
    eacs_e = stack_e[2 * q:3 * q] + stack_e[3 * q:4 * q]
    xdt = xs * dt_e
    xdt_b = xdt.astype(BF16)
    xdte_b = (xdt * dte_e).astype(BF16)
    dec_full = jnp.broadcast_to(jnp.exp(acs_t[:, q - 1:q]), (LANES, SSD_STATE))
    lane = lax.broadcasted_iota(jnp.int32, (q, LANES), 1)
    lo_mask = lane < SSD_HEADDIM
    y_groups = []
    for g in range(SSD_GROUPS):
        gs = slice(g * SSD_STATE, (g + 1) * SSD_STATE)
        gi = slice(g * GROUP_INNER, (g + 1) * GROUP_INNER)
        cb = lax.dot_general(cm[:, gs], bm[:, gs], (((1,), (1,)), ((), ())), preferred_element_type=F32)
        h0 = g * HEADS_PER_GROUP
        hg = h_ref[s, h0:h0 + HEADS_PER_GROUP].reshape(GROUP_INNER, SSD_STATE)
        y_int = lax.dot_general(cm[:, gs], hg.astype(BF16), (((1,), (1,)), ((), ())),
                                preferred_element_type=F32) * eacs_e[:, gi]
        upd = lax.dot_general(xdte_b[:, gi], bm[:, gs], (((0,), (0,)), ((), ())), preferred_element_type=F32)
        y_pairs = []
        for pr in range(HEADS_PER_GROUP // 2):
            col = g * GROUP_INNER + pr * LANES
            xpair = xdt_b[:, col:col + LANES]
            zero = jnp.zeros_like(xpair)
            masks = []
            for hh in range(2):
                h = h0 + 2 * pr + hh
                seg = acs[:, h:h + 1] - acs_t[h:h + 1, :]
                lm = jnp.exp(jnp.where(causal, seg, -jnp.inf))
                masks.append((cb * lm).astype(BF16))
            x_lo = jnp.where(lo_mask, xpair, zero)
            x_hi = jnp.where(lo_mask, zero, xpair)
            if q % LANES == 0:
                y_pairs.append(_dot(jnp.concatenate(masks, axis=1), jnp.concatenate([x_lo, x_hi], axis=0)))
            else:
                y_pairs.append(_dot(masks[0], x_lo) + _dot(masks[1], x_hi))
        y_groups.append(jnp.concatenate(y_pairs, axis=-1) + y_int)
        for hh in range(HEADS_PER_GROUP):
            h = h0 + hh
            h_ref[s, h] = h_ref[s, h] * dec_full[h:h + 1, :] + upd[hh * SSD_HEADDIM:(hh + 1) * SSD_HEADDIM, :]
    y = jnp.concatenate(y_groups, axis=-1) + d_row * xs
    yf = y * _silu(z_ref[pl.ds(row0, q), :])
    normed = []
    for g in range(SSD_GROUPS):
        yg = yf[:, g * GROUP_INNER:(g + 1) * GROUP_INNER]
        normed.append(yg * lax.rsqrt(jnp.mean(yg * yg, axis=-1, keepdims=True) + RMS_EPS))
    yn_ref[pl.ds(row0, q), :] = jnp.concatenate(normed, axis=-1) * nw_row


def _tri_cumsum(tri, da):
    acc = None
    rest = da
    for t in range(3):
        piece = rest.astype(BF16)
        part = _dot(tri, piece)
        acc = part if acc is None else acc + part
        if t < 2:
            rest = rest - piece.astype(F32)
    return acc


def _branch_b_kernel(xbc_ref, z_ref, dtr_ref, g_ref, cprev_ref, h0_ref, wc_ref, bc_ref, dtb_ref, alog_ref,
                     d_ref, nw_ref, exp_ref, wout_ref, o_ref, cst_ref, h_ref, ext_ref, xc_ref, dt_ref, yn_ref,
                     *, nseq, lt, q, valid):
    rows = nseq * lt

    @pl.when(pl.program_id(1) == 0)
    def _():
        h_ref[...] = h0_ref[...]

    conv = _causal_conv(xbc_ref[...].reshape(nseq, lt, SSD_CONV_DIM), cprev_ref, wc_ref[...], cst_ref, ext_ref,
                        valid=valid)
    xc_ref[...] = _silu(conv + bc_ref[...])
    dt = _softplus(dtr_ref[...] + dtb_ref[...])
    if valid < lt:
        t_in_seq = lax.broadcasted_iota(jnp.int32, (nseq, lt, LANES), 1).reshape(rows, LANES)
        dt = jnp.where(t_in_seq < valid, dt, 0.0)
    dt_ref[...] = dt
    a_row = -jnp.exp(alog_ref[...])
    d_row = d_ref[...]
    nw_row = nw_ref[...]
    expand = exp_ref[...]
    nchunk = lt // q

    def body(n, carry):
        _ssd_chunk(n // nchunk, n % nchunk, xc_ref, z_ref, dt_ref, h_ref, yn_ref, a_row, d_row, nw_row, expand,
                   lt=lt, q=q)
        return carry

    lax.fori_loop(0, nseq * nchunk, body, 0)
    o_ref[...] = _sigmoid(g_ref[...]) * _dot(yn_ref[...].astype(BF16), wout_ref[...])


def _branch_b(proj, dtr, cprev, h0, prev_layer, cst_prev, hst_prev, w_conv, b_conv, dt_bias, a_log, d_exp, norm_w,
              expand, w_out, *, layer, nb, ls, nseq, lt, q, valid):
    nj = ls // lt
    rows = nseq * lt
    n_in = 14
    prevs, pspecs, aliases = _alias_args([cst_prev, hst_prev], n_in, 1)
    kern = _skip_alias_refs(functools.partial(_branch_b_kernel, nseq=nseq, lt=lt, q=q, valid=valid),
                            n_in, len(prevs))
    row_blk = lambda i, j: (i * nj + j, 0)
    lay3 = lambda i, j: (layer, 0, 0)
    return pl.pallas_call(
        kern,
        grid=(nb // nseq, nj),
        in_specs=[pl.BlockSpec((rows, SSD_CONV_DIM), lambda i, j: (i * nj + j, COL_XBC // SSD_CONV_DIM)),
                  pl.BlockSpec((rows, SSD_INNER), lambda i, j: (i * nj + j, COL_Z // SSD_INNER)),
                  pl.BlockSpec((rows, LANES), row_blk),
                  pl.BlockSpec((rows, D_MODEL), lambda i, j: (i * nj + j, COL_G // D_MODEL + 1)),
                  pl.BlockSpec((None, nseq, SSD_CONV_W - 1, SSD_CONV_DIM), lambda i, j: (prev_layer, i, 0, 0)),
                  pl.BlockSpec((None, nseq, SSD_HEADS, SSD_HEADDIM, SSD_STATE),
                               lambda i, j: (prev_layer, i, 0, 0, 0)),
                  pl.BlockSpec((None, SSD_CONV_W, SSD_CONV_DIM), lay3),
                  pl.BlockSpec((None, 1, SSD_CONV_DIM), lay3),
                  pl.BlockSpec((None, 1, LANES), lay3),
                  pl.BlockSpec((None, 1, LANES), lay3),
                  pl.BlockSpec((None, 1, SSD_INNER), lay3),
                  pl.BlockSpec((None, 1, SSD_INNER), lay3),
                  pl.BlockSpec((LANES, SSD_INNER), lambda i, j: (0, 0)),
                  pl.BlockSpec((None, SSD_INNER, D_MODEL), lay3)] + pspecs,
        out_specs=[pl.BlockSpec((rows, D_MODEL), row_blk),
                   pl.BlockSpec((None, nseq, SSD_CONV_W - 1, SSD_CONV_DIM), lambda i, j: (layer, i, 0, 0)),
                   pl.BlockSpec((None, nseq, SSD_HEADS, SSD_HEADDIM, SSD_STATE),
                                lambda i, j: (layer, i, 0, 0, 0))],
        out_shape=[jax.ShapeDtypeStruct((nb * ls, D_MODEL), F32),
                   jax.ShapeDtypeStruct((DEPTH, nb, SSD_CONV_W - 1, SSD_CONV_DIM), F32),
                   jax.ShapeDtypeStruct((DEPTH, nb, SSD_HEADS, SSD_HEADDIM, SSD_STATE), F32)],
        scratch_shapes=[pltpu.VMEM((nseq, 2 * SUBLANES, SSD_CONV_DIM), F32),
                        pltpu.VMEM((nseq, lt, SSD_CONV_DIM), F32),
                        pltpu.VMEM((rows, LANES), F32),
                        pltpu.VMEM((rows, SSD_INNER), F32)],
        input_output_aliases=aliases,
        compiler_params=_params("arbitrary", "arbitrary"),
        name="branch_b",
    )(proj, proj, dtr, proj, cprev, h0, w_conv, b_conv, dt_bias, a_log, d_exp, norm_w, expand, w_out, *prevs)


def _merge_kernel(ya_ref, yb_ref, ym_ref, x_ref, wo_ref, g_ref, b_ref, wrh_ref, wrl_ref, br_ref, x1_ref,
                  route_ref):
    merged = (ya_ref[...] + yb_ref[...]) + ym_ref[...]
    mix = _dot(merged.astype(BF16), wo_ref[...])
    x1 = _layer_norm(ALPHA * x_ref[...] + mix, g_ref[...], b_ref[...])
    nslab = D_MODEL // LANES
    for j in range(nslab):
        x1_ref[pl.ds(j, x1.shape[0], stride=nslab), :] = x1[:, j * LANES:(j + 1) * LANES]
    x_hi = x1.astype(BF16)
    x_lo = (x1 - x_hi.astype(F32)).astype(BF16)
    w_hi = wrh_ref[...]
    logits = (_dot(x_hi, w_hi) + (_dot(x_lo, w_hi) + _dot(x_hi, wrl_ref[...]))) + br_ref[...]
    lane = lax.broadcasted_iota(jnp.int32, logits.shape, 1).astype(F32)
    neg = -jnp.inf
    big = float(LANES)
    is_grp = lane < N_EXP_GROUPS
    gl = jnp.where(is_grp, logits, neg)
    gmax = jnp.max(gl, axis=-1, keepdims=True)
    gsum = jnp.sum(jnp.exp(gl - gmax), axis=-1, keepdims=True)
    g_idx = jnp.min(jnp.where(jnp.logical_and(is_grp, gl == gmax), lane, big), axis=-1, keepdims=True)
    g_w = 1.0 / gsum
    e_lo = N_EXP_GROUPS + g_idx * EXP_PER_GROUP
    in_grp = jnp.logical_and(lane >= e_lo, lane < e_lo + EXP_PER_GROUP)
    el = jnp.where(in_grp, logits, neg)
    emax = jnp.max(el, axis=-1, keepdims=True)
    ee = jnp.exp(el - emax)
    esum = jnp.sum(ee, axis=-1, keepdims=True)
    i0 = jnp.min(jnp.where(jnp.logical_and(in_grp, el == emax), lane, big), axis=-1, keepdims=True)
    el2 = jnp.where(lane == i0, neg, el)
    emax2 = jnp.max(el2, axis=-1, keepdims=True)
    i1 = jnp.min(jnp.where(jnp.logical_and(in_grp, el2 == emax2), lane, big), axis=-1, keepdims=True)
    w0 = 1.0 / esum
    w1 = jnp.exp(emax2 - emax) / esum
    wsum = w0 + w1
    gate0 = g_w * w0 / wsum
    gate1 = g_w * w1 / wsum
    id0 = i0 - N_EXP_GROUPS
    id1 = i1 - N_EXP_GROUPS
    route = jnp.where(lane == 0, id0, jnp.where(lane == 1, id1, jnp.where(lane == 2, gate0,
                      jnp.where(lane == 3, gate1, 0.0))))
    route_ref[...] = route


def _merge(ya, yb, ym, x, w_o, ln_g, ln_b, w_r_hi, w_r_lo, b_r, *, layer, tm):
    t = x.shape[0]
    row = lambda i: (i, 0)
    lay = lambda i: (layer, 0, 0)
    return pl.pallas_call(
        _merge_kernel,
        grid=(t // tm,),
        in_specs=[pl.BlockSpec((tm, D_MODEL), row)] * 4 + [
            pl.BlockSpec((None, D_MODEL, D_MODEL), lay),
            pl.BlockSpec((None, 1, D_MODEL), lay), pl.BlockSpec((None, 1, D_MODEL), lay),
            pl.BlockSpec((None, D_MODEL, LANES), lay), pl.BlockSpec((None, D_MODEL, LANES), lay),
            pl.BlockSpec((None, 1, LANES), lay)],
        out_specs=[pl.BlockSpec((tm * (D_MODEL // LANES), LANES), row), pl.BlockSpec((tm, LANES), row)],
        out_shape=[jax.ShapeDtypeStruct((t * (D_MODEL // LANES), LANES), F32),
                   jax.ShapeDtypeStruct((t, LANES), F32)],
        compiler_params=_params("parallel"),
        name="merge_ln_router",
    )(ya, yb, ym, x, w_o, ln_g, ln_b, w_r_hi, w_r_lo, b_r)


def _moe_kernel(bexp_ref, nused_ref, starts_ref, pstarts_ref, counts_ref, bord_ref, elist_ref, nexp_ref,
                tok_ref, row_ref, gate_ref, x_ref, win_hbm, wout_hbm, g_ref, b_ref, o_ref,
                xg_ref, eo_ref, y2_ref, win_buf, wout_buf, wsem, *, nblk, tc, layer):
    c = pl.program_id(0)
    i = pl.program_id(1)
    nslab = D_MODEL // LANES

    def tile(t):
        return pl.ds(pl.multiple_of(t * nslab, nslab), nslab)

    def weight_copies(expert, slot):
        return (pltpu.make_async_copy(win_hbm.at[layer, expert], win_buf.at[slot], wsem.at[slot, 0]),
                pltpu.make_async_copy(wout_hbm.at[layer, expert], wout_buf.at[slot], wsem.at[slot, 1]))

    def fetch(ordinal):
        @pl.when(ordinal < nexp_ref[c])
        def _():
            for cp in weight_copies(elist_ref[c * N_EXPERTS + ordinal], ordinal % MOE_WEIGHT_SLOTS):
                cp.start()

    @pl.when(i < nused_ref[c])
    def _():
        ordinal = bord_ref[c * nblk + i]
        slot = ordinal % MOE_WEIGHT_SLOTS
        first_block = jnp.logical_or(i == 0, bord_ref[jnp.maximum(c * nblk + i - 1, 0)] != ordinal)

        @pl.when(i == 0)
        def _():
            for ahead in range(MOE_WEIGHT_SLOTS - 1):
                fetch(ahead)

        @pl.when(first_block)
        def _():
            fetch(ordinal + MOE_WEIGHT_SLOTS - 1)
            for cp in weight_copies(0, slot):
                cp.wait()

        tbl = c * N_EXPERTS + bexp_ref[c * nblk + i]
        off0 = i * MOE_BLOCK - pstarts_ref[tbl]
        pos0 = starts_ref[tbl] + off0
        nvalid = counts_ref[tbl] - off0

        for r in range(MOE_BLOCK):
            xg_ref[r * nslab:(r + 1) * nslab, :] = x_ref[tile(tok_ref[0, 0, pos0 + r]), :]
        gu = _dot(_token_rows(xg_ref, 0, MOE_BLOCK).astype(BF16), win_buf[slot])
        hid = _silu(gu[:, :EXP_FF]) * gu[:, EXP_FF:]
        eo = _dot(hid.astype(BF16), wout_buf[slot])
        for j in range(nslab):
            eo_ref[pl.ds(j, MOE_BLOCK, stride=nslab), :] = eo[:, j * LANES:(j + 1) * LANES]
        for r in range(MOE_BLOCK):
            row = jnp.where(r < nvalid, row_ref[0, 0, pos0 + r], EXP_TOPK * tc)
            y2_ref[tile(row), :] = gate_ref[0, 0, pos0 + r] * eo_ref[r * nslab:(r + 1) * nslab, :]

    @pl.when(i == nblk - 1)
    def _():
        step = min(MOE_FINISH_ROWS, tc)

        def finish(blk, carry):
            t0 = pl.multiple_of(blk * step, step)
            ffn = _token_rows(y2_ref, t0, step) + _token_rows(y2_ref, tc + t0, step)
            o_ref[pl.ds(t0, step), :] = _layer_norm(ALPHA * _token_rows(x_ref, t0, step) + ffn,
                                                    g_ref[...], b_ref[...])
            return carry

        lax.fori_loop(0, tc // step, finish, 0)


def _token_rows(ref, t0, n):
    nslab = D_MODEL // LANES
    return jnp.concatenate([ref[pl.ds(t0 * nslab + j, n, stride=nslab), :] for j in range(nslab)], axis=-1)


def _moe(x1_tiles, tables, slot_arrays, w_in, w_out, ln_g, ln_b, *, layer, tc, nblk):
    nslab = D_MODEL // LANES
    t = x1_tiles.shape[0] // nslab
    nchunks = t // tc
    s = tc * EXP_TOPK
    kern = functools.partial(_moe_kernel, nblk=nblk, tc=tc, layer=layer)
    smem_row = lambda c, i, *_: (c, 0, 0)
    lay = lambda c, i, *_: (layer, 0, 0)
    grid_spec = pltpu.PrefetchScalarGridSpec(
        num_scalar_prefetch=len(tables),
        grid=(nchunks, nblk),
        in_specs=[pl.BlockSpec((1, 1, s + MOE_BLOCK), smem_row, memory_space=pltpu.SMEM)] * 3 + [
                  pl.BlockSpec((tc * nslab, LANES), lambda c, i, *_: (c, 0), pipeline_mode=pl.Buffered(1)),
                  pl.BlockSpec(memory_space=pl.ANY),
                  pl.BlockSpec(memory_space=pl.ANY),
                  pl.BlockSpec((None, 1, D_MODEL), lay),
                  pl.BlockSpec((None, 1, D_MODEL), lay)],
        out_specs=pl.BlockSpec((tc, D_MODEL), lambda c, i, *_: (c, 0)),
        scratch_shapes=[pltpu.VMEM((MOE_BLOCK * nslab, LANES), F32), pltpu.VMEM((MOE_BLOCK * nslab, LANES), F32),
                        pltpu.VMEM(((EXP_TOPK * tc + 1) * nslab, LANES), F32),
                        pltpu.VMEM((MOE_WEIGHT_SLOTS, D_MODEL, 2 * EXP_FF), BF16),
                        pltpu.VMEM((MOE_WEIGHT_SLOTS, EXP_FF, D_MODEL), BF16),
                        pltpu.SemaphoreType.DMA((MOE_WEIGHT_SLOTS, 2))],
    )
    return pl.pallas_call(
        kern,
        grid_spec=grid_spec,
        out_shape=jax.ShapeDtypeStruct((t, D_MODEL), F32),
        compiler_params=_params("arbitrary", "arbitrary"),
        name="moe_experts",
    )(*tables, *[a[:, None, :] for a in slot_arrays], x1_tiles, w_in, w_out, ln_g, ln_b)


def _dispatch(route, *, tc, nblk):
    t = route.shape[0]
    nchunks = t // tc
    s = tc * EXP_TOPK
    ids = route[:, 0:EXP_TOPK].astype(jnp.int32).reshape(nchunks, s)
    gates = route[:, EXP_TOPK:2 * EXP_TOPK].reshape(nchunks, s)
    slot = jnp.broadcast_to(jnp.arange(s, dtype=jnp.int32)[None, :], (nchunks, s))
    _, order, gate_sorted = lax.sort((ids, slot, gates), dimension=1, is_stable=True, num_keys=1)
    tok_sorted = order // EXP_TOPK
    row_sorted = (order % EXP_TOPK) * tc + tok_sorted
    counts =jnp.sum((ids[:, :, None] == jnp.arange(N_EXPERTS, dtype=jnp.int32)).astype(jnp.int32), axis=1)
    starts = jnp.cumsum(counts, axis=-1) - counts
    padded = (counts + MOE_BLOCK - 1) // MOE_BLOCK * MOE_BLOCK
    pends = jnp.cumsum(padded, axis=-1)
    pstarts = pends - padded
    blk_start = jnp.arange(nblk, dtype=jnp.int32) * MOE_BLOCK
    bexp = jnp.sum((pends[:, None, :] <= blk_start[None, :, None]).astype(jnp.int32), axis=-1)
    bexp = jnp.minimum(bexp, N_EXPERTS - 1).reshape(-1)
    nused = pends[:, -1] // MOE_BLOCK
    nonempty = counts > 0
    nexp = jnp.sum(nonempty.astype(jnp.int32), axis=-1)
    ordinal_of_expert = jnp.cumsum(nonempty.astype(jnp.int32), axis=-1) - 1
    elist = jnp.argsort(jnp.logical_not(nonempty), axis=-1, stable=True).astype(jnp.int32)
    bexp_2d = bexp.reshape(nchunks, nblk)
    bord = jnp.sum(jnp.where(bexp_2d[:, :, None] == jnp.arange(N_EXPERTS, dtype=jnp.int32),
                             ordinal_of_expert[:, None, :], 0), axis=-1).astype(jnp.int32)
    tables = (bexp, nused, starts.reshape(-1), pstarts.reshape(-1), counts.reshape(-1), bord.reshape(-1),
              elist.reshape(-1), nexp)
    pad = ((0, 0), (0, MOE_BLOCK))
    return tables, (jnp.pad(tok_sorted, pad), jnp.pad(row_sorted, pad), jnp.pad(gate_sorted, pad))


def _prep_weights(w_in, w_sconv_out, w_ssd_out, w_mem_k, w_mem_v, w_mem_out, w_o, w_exp_in, w_exp_out,
                  w_router_grp, b_router_grp, w_router_exp, b_router_exp, ssd_dt_bias, ssd_a_log, ssd_d):
    o_b, o_z, o_xbc, o_dt, o_q, o_g = 0, 3072, 5120, 8192, 8224, 9248
    w_main = jnp.concatenate([w_in[:, :, o_xbc:o_dt], w_in[:, :, o_b:o_z], w_in[:, :, o_z:o_xbc],
                              w_in[:, :, o_q:o_g], w_in[:, :, o_g:]], axis=-1).astype(BF16)
    pad_h = LANES - SSD_HEADS
    w_dt = jnp.pad(w_in[:, :, o_dt:o_q], ((0, 0), (0, 0), (0, pad_h))).astype(BF16)
    pad_r = LANES - N_EXP_GROUPS - N_EXPERTS
    w_r = jnp.pad(jnp.concatenate([w_router_grp, w_router_exp], axis=-1), ((0, 0), (0, 0), (0, pad_r)))
    w_r_hi = w_r.astype(BF16)
    w_r_lo = (w_r - w_r_hi.astype(F32)).astype(BF16)
    b_r = jnp.pad(jnp.concatenate([b_router_grp, b_router_exp], axis=-1), ((0, 0), (0, pad_r)))[:, None, :]
    head_of_col = jnp.arange(SSD_INNER, dtype=jnp.int32) // SSD_HEADDIM
    expand = (jnp.arange(LANES, dtype=jnp.int32)[:, None] == head_of_col[None, :]).astype(BF16)
    return dict(
        w_main=w_main, w_dt=w_dt, w_r_hi=w_r_hi, w_r_lo=w_r_lo, b_r=b_r, expand=expand,
        w_sconv_out=w_sconv_out.astype(BF16), w_ssd_out=w_ssd_out.astype(BF16),
        w_mem_kv=jnp.concatenate([w_mem_k, w_mem_v], axis=-1).astype(BF16),
        w_mem_out=w_mem_out.astype(BF16), w_o=w_o.astype(BF16),
        w_exp_in=w_exp_in.astype(BF16), w_exp_out=w_exp_out.astype(BF16),
        dt_bias=jnp.pad(ssd_dt_bias, ((0, 0), (0, pad_h)))[:, None, :],
        a_log=jnp.pad(ssd_a_log, ((0, 0), (0, pad_h)))[:, None, :],
        d_exp=jnp.repeat(ssd_d, SSD_HEADDIM, axis=-1)[:, None, :],
    )


def _moe_blocks(tc):
    return (tc * EXP_TOPK + N_EXPERTS * (MOE_BLOCK - 1) + MOE_BLOCK - 1) // MOE_BLOCK


def _decoder_layer(l, x, mem_k, mem_v, states_in, prev_layer, states_out, wp, p, cfg):
    nb, ls = cfg["nb"], cfg["ls"]
    sc_in, cv_in, h_in = states_in
    sc_out, cv_out, h_out = states_out
    proj = _matmul(x, wp["w_main"], l, cfg["tm_proj"], 2048)
    dtr = _matmul(x, wp["w_dt"], l, cfg["tm_proj"], LANES)
    ya, sc_new = _branch_a(proj, sc_in, prev_layer, sc_out, p["w_sconv"], wp["w_sconv_out"],
                           layer=l, nb=nb, ls=ls, nseq=cfg["a_nseq"], lt=cfg["a_lt"], valid=cfg["a_valid"])
    yb, cv_new, h_new = _branch_b(proj, dtr, cv_in, h_in, prev_layer, cv_out, h_out, p["w_ssd_conv"],
                                  p["b_ssd_conv"], wp["dt_bias"], wp["a_log"], wp["d_exp"], p["ssd_norm_w"],
                                  wp["expand"], wp["w_ssd_out"],
                                  layer=l, nb=nb, ls=ls, nseq=cfg["b_nseq"], lt=cfg["b_lt"], q=cfg["b_q"],
                                  valid=cfg["b_valid"])
    ym = _branch_m(proj, mem_k, mem_v, wp["w_mem_out"], layer=l, nb=nb, ls=ls, nseq=cfg["m_nseq"],
                   lt=cfg["m_lt"])
    x1, route = _merge(ya, yb, ym, x, wp["w_o"], p["ln1_g"], p["ln1_b"], wp["w_r_hi"], wp["w_r_lo"], wp["b_r"],
                       layer=l, tm=cfg["tm_merge"])
    tc = cfg["moe_tc"]
    nblk = _moe_blocks(tc)
    tables, slot_arrays = _dispatch(route, tc=tc, nblk=nblk)
    x2 = _moe(x1, tables, slot_arrays, wp["w_exp_in"], wp["w_exp_out"], p["ln2_g"], p["ln2_b"],
              layer=l, tc=tc, nblk=nblk)
    return x2, (sc_new, cv_new, h_new)


PROMPT_CFG = dict(tm_proj=1024, a_nseq=1, a_lt=512, b_nseq=1, b_lt=256, b_q=SSD_CHUNK, m_nseq=1, m_lt=512,
                  tm_merge=512, moe_tc=2048)
SAMPLE_PAD = SUBLANES


def _zero_states(nb):
    return (jnp.zeros((DEPTH, nb, SC_CONV_W - 1, SC_DIM), F32),
            jnp.zeros((DEPTH, nb, SSD_CONV_W - 1, SSD_CONV_DIM), F32),
            jnp.zeros((DEPTH, nb, SSD_HEADS, SSD_HEADDIM, SSD_STATE), F32))


def kernel(x_prompt, x_sample, mem_prompt, cache_mem_k, cache_mem_v, state_sconv, state_ssd_conv, state_ssd, w_in, w_sconv, w_sconv_out, w_ssd_conv, b_ssd_conv, ssd_dt_bias, ssd_a_log, ssd_d, ssd_norm_w, w_ssd_out, w_mem_k, w_mem_v, w_mem_out, w_o, ln1_g, ln1_b, w_router_grp, b_router_grp, w_router_exp, b_router_exp, w_exp_in, w_exp_out, ln2_g, ln2_b):
    bp, seq, _ = x_prompt.shape
    bs, dec_seq, _ = x_sample.shape
    wp = _prep_weights(w_in, w_sconv_out, w_ssd_out, w_mem_k, w_mem_v, w_mem_out, w_o, w_exp_in, w_exp_out,
                       w_router_grp, b_router_grp, w_router_exp, b_router_exp, ssd_dt_bias, ssd_a_log, ssd_d)
    p = dict(w_sconv=w_sconv, w_ssd_conv=w_ssd_conv, b_ssd_conv=b_ssd_conv[:, None, :],
             ssd_norm_w=ssd_norm_w[:, None, :], ln1_g=ln1_g[:, None, :], ln1_b=ln1_b[:, None, :],
             ln2_g=ln2_g[:, None, :], ln2_b=ln2_b[:, None, :])

    cfg_p = dict(PROMPT_CFG, nb=bp, ls=seq, a_valid=PROMPT_CFG["a_lt"], b_valid=PROMPT_CFG["b_lt"])
    mem_rows = mem_prompt.reshape(bp * MEM_LEN, D_MODEL)
    zeros_in = (jnp.zeros((1, bp, SC_CONV_W - 1, SC_DIM), F32),
                jnp.zeros((1, bp, SSD_CONV_W - 1, SSD_CONV_DIM), F32),
                jnp.zeros((1, bp, SSD_HEADS, SSD_HEADDIM, SSD_STATE), F32))
    yp = x_prompt.reshape(bp * seq, D_MODEL)
    mk_p = jnp.zeros((DEPTH, bp, MEM_LEN, MEM_HEADS, MEM_HEAD_DIM), F32)
    mv_p = jnp.zeros((DEPTH, bp, MEM_LEN, MEM_HEADS, MEM_HEAD_DIM), F32)
    st_p = _zero_states(bp)
    for l in range(DEPTH):
        mk_p = _memkv(mem_rows, wp["w_mem_kv"], l, 0, mk_p, nb=bp, nbt=4)
        mv_p = _memkv(mem_rows, wp["w_mem_kv"], l, 1, mv_p, nb=bp, nbt=4)
        yp, st_p = _decoder_layer(l, yp, mk_p, mv_p, zeros_in, 0, st_p, wp, p, cfg_p)

    ls = SAMPLE_PAD
    cfg_s = dict(tm_proj=bs * ls, nb=bs, ls=ls, a_nseq=16, a_lt=ls, a_valid=dec_seq, b_nseq=4, b_lt=ls, b_q=ls,
                 b_valid=dec_seq, m_nseq=2, m_lt=ls, tm_merge=512, moe_tc=bs * ls)
    ys = jnp.pad(x_sample, ((0, 0), (0, ls - dec_seq), (0, 0))).reshape(bs * ls, D_MODEL)
    st_s = _zero_states(bs)
    for l in range(DEPTH):
        ys, st_s = _decoder_layer(l, ys, cache_mem_k, cache_mem_v, (state_sconv, state_ssd_conv, state_ssd), l,
                                  st_s, wp, p, cfg_s)

    y_prompt = yp.reshape(bp, seq, D_MODEL)
    y_sample = ys.reshape(bs, ls, D_MODEL)[:, :dec_seq]
    return (y_prompt, y_sample, mk_p, mv_p, st_p[0], st_p[1], st_p[2], st_s[0], st_s[1], st_s[2])
```

```python
import functools

import jax
import jax.numpy as jnp
from jax import lax
from jax.experimental import pallas as pl
from jax.experimental.pallas import tpu as pltpu

F32 = jnp.float32
BF16 = jnp.bfloat16

D_MODEL = 1024
DEPTH = 4
SC_DIM = D_MODEL
SC_CONV_W = 3
SSD_INNER = 2 * D_MODEL
SSD_HEADDIM = 64
SSD_HEADS = SSD_INNER // SSD_HEADDIM
SSD_GROUPS = 4
SSD_STATE = 128
SSD_CONV_W = 4
SSD_BC = SSD_GROUPS * SSD_STATE
SSD_CONV_DIM = SSD_INNER + 2 * SSD_BC
SSD_CHUNK = 128
HEADS_PER_GROUP = SSD_HEADS // SSD_GROUPS
GROUP_INNER = SSD_INNER // SSD_GROUPS
MEM_LEN = 256
MEM_HEADS = 4
MEM_HEAD_DIM = D_MODEL // MEM_HEADS
N_EXP_GROUPS = 4
EXP_PER_GROUP = 8
N_EXPERTS = N_EXP_GROUPS * EXP_PER_GROUP
EXP_TOPK = 2
EXP_FF = 512
MOE_BLOCK = 128
MOE_FINISH_ROWS = 256
MOE_WEIGHT_SLOTS = 3
ALPHA = (2 * DEPTH) ** 0.25
LN_EPS = 1e-5
RMS_EPS = 1e-6

LANES = 128
SUBLANES = 8
VMEM_LIMIT = 56 * 1024 * 1024

COL_XBC = 0
COL_BCX = SSD_CONV_DIM
COL_Z = COL_BCX + 3 * SC_DIM
COL_Q = COL_Z + SSD_INNER
COL_G = COL_Q + D_MODEL
PROJ_COLS = COL_G + 3 * D_MODEL


def _dot(a, b):
    return jnp.dot(a, b, preferred_element_type=F32)


def _split_dot(a, b, terms, dims=(((1,), (0,)), ((), ()))):
    acc = None
    rest = a
    for t in range(terms):
        piece = rest.astype(BF16)
        part = lax.dot_general(piece, b, dims, preferred_element_type=F32)
        acc = part if acc is None else acc + part
        if t + 1 < terms:
            rest = rest - piece.astype(F32)
    return acc


def _sigmoid(x):
    return 1.0 / (1.0 + jnp.exp(-x))


def _silu(x):
    return x * _sigmoid(x)


def _layer_norm(x, g, b):
    mu = jnp.mean(x, axis=-1, keepdims=True)
    xc = x - mu
    var = jnp.mean(xc * xc, axis=-1, keepdims=True)
    return xc * lax.rsqrt(var + LN_EPS) * g + b


def _params(*sem):
    return pltpu.CompilerParams(dimension_semantics=sem, vmem_limit_bytes=VMEM_LIMIT)


def _skip_alias_refs(kernel_fn, n_in, n_alias):
    def wrapped(*refs):
        return kernel_fn(*refs[:n_in], *refs[n_in + n_alias:])
    return wrapped


def _alias_args(prevs, n_in, first_out):
    prevs = [p for p in prevs if p is not None]
    specs = [pl.BlockSpec(memory_space=pl.ANY)] * len(prevs)
    aliases = {n_in + k: first_out + k for k in range(len(prevs))}
    return prevs, specs, aliases


def _mm_kernel(x_ref, w_ref, o_ref):
    o_ref[...] = _dot(x_ref[...].astype(BF16), w_ref[...])


def _matmul(x, w, layer, tm, tn):
    m, k = x.shape
    n = w.shape[2]
    tm = min(tm, m)
    tn = min(tn, n)
    return pl.pallas_call(
        _mm_kernel,
        grid=(m // tm, n // tn),
        in_specs=[pl.BlockSpec((tm, k), lambda i, j: (i, 0)),
                  pl.BlockSpec((None, k, tn), lambda i, j: (layer, 0, j))],
        out_specs=pl.BlockSpec((tm, tn), lambda i, j: (i, j)),
        out_shape=jax.ShapeDtypeStruct((m, n), F32),
        compiler_params=_params("parallel", "arbitrary"),
        name="matmul",
    )(x, w)


def _memkv_kernel(x_ref, w_ref, o_ref, *, nbt):
    res = _dot(x_ref[...].astype(BF16), w_ref[...])
    for h in range(MEM_HEADS):
        o_ref[:, :, h, :] = res[:, h * MEM_HEAD_DIM:(h + 1) * MEM_HEAD_DIM].reshape(nbt, MEM_LEN, MEM_HEAD_DIM)


def _memkv(mem_rows, w_kv, layer, which, stacked_prev, *, nb, nbt):
    prevs, pspecs, aliases = _alias_args([stacked_prev], 2, 0)
    kern = _skip_alias_refs(functools.partial(_memkv_kernel, nbt=nbt), 2, len(prevs))
    return pl.pallas_call(
        kern,
        grid=(nb // nbt,),
        in_specs=[pl.BlockSpec((nbt * MEM_LEN, D_MODEL), lambda i: (i, 0)),
                  pl.BlockSpec((None, D_MODEL, D_MODEL), lambda i: (layer, 0, which))] + pspecs,
        out_specs=pl.BlockSpec((None, nbt, MEM_LEN, MEM_HEADS, MEM_HEAD_DIM), lambda i: (layer, i, 0, 0, 0)),
        out_shape=jax.ShapeDtypeStruct((DEPTH, nb, MEM_LEN, MEM_HEADS, MEM_HEAD_DIM), F32),
        input_output_aliases=aliases,
        compiler_params=_params("parallel"),
        name="mem_kv",
    )(mem_rows, w_kv, *prevs)


def _shift_rows(u, tail, d):
    nseq, lt, c_dim = u.shape
    groups = lt // SUBLANES
    row = lax.broadcasted_iota(jnp.int32, (SUBLANES, c_dim), 0)
    outs = []
    for s in range(nseq):
        rot = pltpu.roll(u[s].reshape(groups, SUBLANES, c_dim), d, axis=1)
        before = pltpu.roll(tail[s], d, axis=0)[None]
        if groups > 1:
            before = jnp.concatenate([before, rot[:groups - 1]], axis=0)
        outs.append(jnp.where(row < d, before, rot).reshape(lt, c_dim))
    return jnp.stack(outs, axis=0)


def _causal_conv(u, prev_ref, w, st_ref, ext_ref, *, valid):
    nseq, lt, _ = u.shape
    width = w.shape[0]
    halo = width - 1

    @pl.when(pl.program_id(1) == 0)
    def _():
        ext_ref[:, 0:SUBLANES, :] = jnp.zeros_like(ext_ref[:, 0:SUBLANES, :])
        ext_ref[:, SUBLANES - halo:SUBLANES, :] = prev_ref[...]

    tail = ext_ref[:, 0:SUBLANES, :]
    last8 = u[:, lt - SUBLANES:lt, :]
    ext_ref[:, SUBLANES:2 * SUBLANES, :] = last8
    out = _shift_rows(u, tail, halo) * w[0:1, :]
    for k in range(1, halo):
        out = out + _shift_rows(u, tail, halo - k) * w[k:k + 1, :]
    out = out + u * w[halo:width, :]
    end = 2 * SUBLANES - (lt - valid)
    st_ref[...] = ext_ref[:, end - halo:end, :]
    ext_ref[:, 0:SUBLANES, :] = last8
    return out


def _branch_a_kernel(bcx_ref, g_ref, prev_ref, wc_ref, wout_ref, o_ref, st_ref, ext_ref, *, nseq, lt, valid):
    c_dim = SC_DIM
    rows = nseq * lt
    bcx = bcx_ref[...]
    scb = bcx[:, :c_dim]
    u = (bcx[:, c_dim:2 * c_dim] * bcx[:, 2 * c_dim:]).reshape(nseq, lt, c_dim)
    v = _causal_conv(u, prev_ref, wc_ref[...], st_ref, ext_ref, valid=valid)
    ya = _dot((scb * v.reshape(rows, c_dim)).astype(BF16), wout_ref[...])
    o_ref[...] = _sigmoid(g_ref[...]) * ya


def _branch_a(proj, prev, prev_layer, st_prev, w_conv, w_out, *, layer, nb, ls, nseq, lt, valid):
    nj = ls // lt
    rows = nseq * lt
    n_in = 5
    prevs, pspecs, aliases = _alias_args([st_prev], n_in, 1)
    kern = _skip_alias_refs(functools.partial(_branch_a_kernel, nseq=nseq, lt=lt, valid=valid), n_in, len(prevs))
    return pl.pallas_call(
        kern,
        grid=(nb // nseq, nj),
        in_specs=[pl.BlockSpec((rows, 3 * SC_DIM), lambda i, j: (i * nj + j, COL_BCX // (3 * SC_DIM))),
                  pl.BlockSpec((rows, D_MODEL), lambda i, j: (i * nj + j, COL_G // D_MODEL)),
                  pl.BlockSpec((None, nseq, SC_CONV_W - 1, SC_DIM), lambda i, j: (prev_layer, i, 0, 0)),
                  pl.BlockSpec((None, SC_CONV_W, SC_DIM), lambda i, j: (layer, 0, 0)),
                  pl.BlockSpec((None, SC_DIM, D_MODEL), lambda i, j: (layer, 0, 0))] + pspecs,
        out_specs=[pl.BlockSpec((rows, D_MODEL), lambda i, j: (i * nj + j, 0)),
                   pl.BlockSpec((None, nseq, SC_CONV_W - 1, SC_DIM), lambda i, j: (layer, i, 0, 0))],
        out_shape=[jax.ShapeDtypeStruct((nb * ls, D_MODEL), F32),
                   jax.ShapeDtypeStruct((DEPTH, nb, SC_CONV_W - 1, SC_DIM), F32)],
        scratch_shapes=[pltpu.VMEM((nseq, 2 * SUBLANES, SC_DIM), F32)],
        input_output_aliases=aliases,
        compiler_params=_params("arbitrary", "arbitrary"),
        name="branch_a",
    )(proj, proj, prev, w_conv, w_out, *prevs)


def _load_head(kv_ref, h):
    slabs = [kv_ref[:, :, h, c * LANES:(c + 1) * LANES] for c in range(MEM_HEAD_DIM // LANES)]
    return jnp.concatenate(slabs, axis=-1).astype(BF16)


def _branch_m_kernel(q_ref, g_ref, k_ref, v_ref, wout_ref, o_ref, *, nseq, lt):
    rows = nseq * lt
    scale = MEM_HEAD_DIM ** -0.5
    qb = q_ref[...].astype(BF16).reshape(nseq, lt, D_MODEL)
    outs = []
    for h in range(MEM_HEADS):
        sl = slice(h * MEM_HEAD_DIM, (h + 1) * MEM_HEAD_DIM)
        kh = _load_head(k_ref, h)
        vh = _load_head(v_ref, h)
        s = jnp.einsum("sld,smd->slm", qb[:, :, sl], kh, preferred_element_type=F32) * scale
        s = s - jnp.max(s, axis=-1, keepdims=True)
        p = jnp.exp(s)
        p = p / jnp.sum(p, axis=-1, keepdims=True)
        outs.append(jnp.einsum("slm,smd->sld", p.astype(BF16), vh, preferred_element_type=F32))
    o = jnp.concatenate(outs, axis=-1).reshape(rows, D_MODEL)
    o_ref[...] = _sigmoid(g_ref[...]) * _dot(o.astype(BF16), wout_ref[...])


def _branch_m(proj, mem_k, mem_v, w_out, *, layer, nb, ls, nseq, lt):
    nj = ls // lt
    rows = nseq * lt
    kern = functools.partial(_branch_m_kernel, nseq=nseq, lt=lt)
    kv_spec = pl.BlockSpec((None, nseq, MEM_LEN, MEM_HEADS, MEM_HEAD_DIM), lambda i, j: (layer, i, 0, 0, 0))
    return pl.pallas_call(
        kern,
        grid=(nb // nseq, nj),
        in_specs=[pl.BlockSpec((rows, D_MODEL), lambda i, j: (i * nj + j, COL_Q // D_MODEL)),
                  pl.BlockSpec((rows, D_MODEL), lambda i, j: (i * nj + j, COL_G // D_MODEL + 2)),
                  kv_spec, kv_spec,
                  pl.BlockSpec((None, D_MODEL, D_MODEL), lambda i, j: (layer, 0, 0))],
        out_specs=pl.BlockSpec((rows, D_MODEL), lambda i, j: (i * nj + j, 0)),
        out_shape=jax.ShapeDtypeStruct((nb * ls, D_MODEL), F32),
        compiler_params=_params("parallel", "arbitrary"),
        name="branch_m",
    )(proj, proj, mem_k, mem_v, w_out)


def _softplus(x):
    return jnp.maximum(x, 0.0) + jnp.log1p(jnp.exp(-jnp.abs(x)))


def _ssd_chunk(s, c, xc_ref, z_ref, dt_ref, h_ref, yn_ref, a_row, d_row, nw_row, expand, *, lt, q):
    r0 = pl.multiple_of(c * q, q)
    row0 = pl.multiple_of(s * lt + c * q, q)
    xs = xc_ref[s, pl.ds(r0, q), 0:SSD_INNER]
    bm = xc_ref[s, pl.ds(r0, q), SSD_INNER:SSD_INNER + SSD_BC].astype(BF16)
    cm = xc_ref[s, pl.ds(r0, q), SSD_INNER + SSD_BC:SSD_CONV_DIM].astype(BF16)
    dt = dt_ref[pl.ds(row0, q), :]
    da = dt * a_row
    ti = lax.broadcasted_iota(jnp.int32, (q, q), 0)
    si = lax.broadcasted_iota(jnp.int32, (q, q), 1)
    causal = si <= ti
    tri = jnp.where(causal, 1.0, 0.0).astype(BF16)
    acs = _tri_cumsum(tri, da)
    acs_t = _split_dot(da, tri, 3, dims=(((0,), (1,)), ((), ())))
    acs_last = acs[q - 1:q, :]
    eacs = jnp.exp(acs)
    eacs_hi = eacs.astype(BF16).astype(F32)
    stack = jnp.concatenate([dt, jnp.exp(acs_last - acs), eacs_hi, eacs - eacs_hi], axis=0)
    stack_e = _dot(stack.astype(BF16), expand)
    dt_e, dte_e = stack_e[0:q], stack_e[q:2 * q]
    eacs_e = stack_e[2 * q:3 * q] + stack_e[3 * q:4 * q]
    xdt = xs * dt_e
    xdt_b = xdt.astype(BF16)
    xdte_b = (xdt * dte_e).astype(BF16)
    dec_full = jnp.broadcast_to(jnp.exp(acs_t[:, q - 1:q]), (LANES, SSD_STATE))
    lane = lax.broadcasted_iota(jnp.int32, (q, LANES), 1)
    lo_mask = lane < SSD_HEADDIM
    y_groups = []
    for g in range(SSD_GROUPS):
        gs = slice(g * SSD_STATE, (g + 1) * SSD_STATE)
        gi = slice(g * GROUP_INNER, (g + 1) * GROUP_INNER)
        cb = lax.dot_general(cm[:, gs], bm[:, gs], (((1,), (1,)), ((), ())), preferred_element_type=F32)
        h0 = g * HEADS_PER_GROUP
        hg = h_ref[s, h0:h0 + HEADS_PER_GROUP].reshape(GROUP_INNER, SSD_STATE)
        y_int = lax.dot_general(cm[:, gs], hg.astype(BF16), (((1,), (1,)), ((), ())),
                                preferred_element_type=F32) * eacs_e[:, gi]
        upd = lax.dot_general(xdte_b[:, gi], bm[:, gs], (((0,), (0,)), ((), ())), preferred_element_type=F32)
        y_pairs = []
        for pr in range(HEADS_PER_GROUP // 2):
            col = g * GROUP_INNER + pr * LANES
            xpair = xdt_b[:, col:col + LANES]
            zero = jnp.zeros_like(xpair)
            masks = []
            for hh in range(2):
                h = h0 + 2 * pr + hh
                seg = acs[:, h:h + 1] - acs_t[h:h + 1, :]
                lm = jnp.exp(jnp.where(causal, seg, -jnp.inf))
                masks.append((cb * lm).astype(BF16))
            x_lo = jnp.where(lo_mask, xpair, zero)
            x_hi = jnp.where(lo_mask, zero, xpair)
            if q % LANES == 0:
                y_pairs.append(_dot(jnp.concatenate(masks, axis=1), jnp.concatenate([x_lo, x_hi], axis=0)))
            else:
                y_pairs.append(_dot(masks[0], x_lo) + _dot(masks[1], x_hi))
        y_groups.append(jnp.concatenate(y_pairs, axis=-1) + y_int)
        for hh in range(HEADS_PER_GROUP):
            h = h0 + hh
            h_ref[s, h] = h_ref[s, h] * dec_full[h:h + 1, :] + upd[hh * SSD_HEADDIM:(hh + 1) * SSD_HEADDIM, :]
    y = jnp.concatenate(y_groups, axis=-1) + d_row * xs
    yf = y * _silu(z_ref[pl.ds(row0, q), :])
    normed = []
    for g in range(SSD_GROUPS):
        yg = yf[:, g * GROUP_INNER:(g + 1) * GROUP_INNER]
        normed.append(yg * lax.rsqrt(jnp.mean(yg * yg, axis=-1, keepdims=True) + RMS_EPS))
    yn_ref[pl.ds(row0, q), :] = jnp.concatenate(normed, axis=-1) * nw_row


def _tri_cumsum(tri, da):
    acc = None
    rest = da
    for t in range(3):
        piece = rest.astype(BF16)
        part = _dot(tri, piece)
        acc = part if acc is None else acc + part
        if t < 2:
            rest = rest - piece.astype(F32)
    return acc


def _branch_b_kernel(xbc_ref, z_ref, dtr_ref, g_ref, cprev_ref, h0_ref, wc_ref, bc_ref, dtb_ref, alog_ref,
                     d_ref, nw_ref, exp_ref, wout_ref, o_ref, cst_ref, h_ref, ext_ref, xc_ref, dt_ref, yn_ref,
                     *, nseq, lt, q, valid):
    rows = nseq * lt

    @pl.when(pl.program_id(1) == 0)
    def _():
        h_ref[...] = h0_ref[...]

    conv = _causal_conv(xbc_ref[...].reshape(nseq, lt, SSD_CONV_DIM), cprev_ref, wc_ref[...], cst_ref, ext_ref,
                        valid=valid)
    xc_ref[...] = _silu(conv + bc_ref[...])
    dt = _softplus(dtr_ref[...] + dtb_ref[...])
    if valid < lt:
        t_in_seq = lax.broadcasted_iota(jnp.int32, (nseq, lt, LANES), 1).reshape(rows, LANES)
        dt = jnp.where(t_in_seq < valid, dt, 0.0)
    dt_ref[...] = dt
    a_row = -jnp.exp(alog_ref[...])
    d_row = d_ref[...]
    nw_row = nw_ref[...]
    expand = exp_ref[...]
    nchunk = lt // q

    def body(n, carry):
        _ssd_chunk(n // nchunk, n % nchunk, xc_ref, z_ref, dt_ref, h_ref, yn_ref, a_row, d_row, nw_row, expand,
                   lt=lt, q=q)
        return carry

    lax.fori_loop(0, nseq * nchunk, body, 0)
    o_ref[...] = _sigmoid(g_ref[...]) * _dot(yn_ref[...].astype(BF16), wout_ref[...])


def _branch_b(proj, dtr, cprev, h0, prev_layer, cst_prev, hst_prev, w_conv, b_conv, dt_bias, a_log, d_exp, norm_w,
              expand, w_out, *, layer, nb, ls, nseq, lt, q, valid):
    nj = ls // lt
    rows = nseq * lt
    n_in = 14
    prevs, pspecs, aliases = _alias_args([cst_prev, hst_prev], n_in, 1)
    kern = _skip_alias_refs(functools.partial(_branch_b_kernel, nseq=nseq, lt=lt, q=q, valid=valid),
                            n_in, len(prevs))
    row_blk = lambda i, j: (i * nj + j, 0)
    lay3 = lambda i, j: (layer, 0, 0)
    return pl.pallas_call(
        kern,
        grid=(nb // nseq, nj),
        in_specs=[pl.BlockSpec((rows, SSD_CONV_DIM), lambda i, j: (i * nj + j, COL_XBC // SSD_CONV_DIM)),
                  pl.BlockSpec((rows, SSD_INNER), lambda i, j: (i * nj + j, COL_Z // SSD_INNER)),
                  pl.BlockSpec((rows, LANES), row_blk),
                  pl.BlockSpec((rows, D_MODEL), lambda i, j: (i * nj + j, COL_G // D_MODEL + 1)),
                  pl.BlockSpec((None, nseq, SSD_CONV_W - 1, SSD_CONV_DIM), lambda i, j: (prev_layer, i, 0, 0)),
                  pl.BlockSpec((None, nseq, SSD_HEADS, SSD_HEADDIM, SSD_STATE),
                               lambda i, j: (prev_layer, i, 0, 0, 0)),
                  pl.BlockSpec((None, SSD_CONV_W, SSD_CONV_DIM), lay3),
                  pl.BlockSpec((None, 1, SSD_CONV_DIM), lay3),
                  pl.BlockSpec((None, 1, LANES), lay3),
                  pl.BlockSpec((None, 1, LANES), lay3),
                  pl.BlockSpec((None, 1, SSD_INNER), lay3),
                  pl.BlockSpec((None, 1, SSD_INNER), lay3),
                  pl.BlockSpec((LANES, SSD_INNER), lambda i, j: (0, 0)),
                  pl.BlockSpec((None, SSD_INNER, D_MODEL), lay3)] + pspecs,
        out_specs=[pl.BlockSpec((rows, D_MODEL), row_blk),
                   pl.BlockSpec((None, nseq, SSD_CONV_W - 1, SSD_CONV_DIM), lambda i, j: (layer, i, 0, 0)),
                   pl.BlockSpec((None, nseq, SSD_HEADS, SSD_HEADDIM, SSD_STATE),
                                lambda i, j: (layer, i, 0, 0, 0))],
        out_shape=[jax.ShapeDtypeStruct((nb * ls, D_MODEL), F32),
                   jax.ShapeDtypeStruct((DEPTH, nb, SSD_CONV_W - 1, SSD_CONV_DIM), F32),
                   jax.ShapeDtypeStruct((DEPTH, nb, SSD_HEADS, SSD_HEADDIM, SSD_STATE), F32)],
        scratch_shapes=[pltpu.VMEM((nseq, 2 * SUBLANES, SSD_CONV_DIM), F32),
                        pltpu.VMEM((nseq, lt, SSD_CONV_DIM), F32),
                        pltpu.VMEM((rows, LANES), F32),
                        pltpu.VMEM((rows, SSD_INNER), F32)],
        input_output_aliases=aliases,
        compiler_params=_params("arbitrary", "arbitrary"),
        name="branch_b",
    )(proj, proj, dtr, proj, cprev, h0, w_conv, b_conv, dt_bias, a_log, d_exp, norm_w, expand, w_out, *prevs)


def _merge_kernel(ya_ref, yb_ref, ym_ref, x_ref, wo_ref, g_ref, b_ref, wrh_ref, wrl_ref, br_ref, x1_ref,
                  route_ref):
    merged = (ya_ref[...] + yb_ref[...]) + ym_ref[...]
    mix = _dot(merged.astype(BF16), wo_ref[...])
    x1 = _layer_norm(ALPHA * x_ref[...] + mix, g_ref[...], b_ref[...])
    nslab = D_MODEL // LANES
    for j in range(nslab):
        x1_ref[pl.ds(j, x1.shape[0], stride=nslab), :] = x1[:, j * LANES:(j + 1) * LANES]
    x_hi = x1.astype(BF16)
    x_lo = (x1 - x_hi.astype(F32)).astype(BF16)
    w_hi = wrh_ref[...]
    logits = (_dot(x_hi, w_hi) + (_dot(x_lo, w_hi) + _dot(x_hi, wrl_ref[...]))) + br_ref[...]
    lane = lax.broadcasted_iota(jnp.int32, logits.shape, 1).astype(F32)
    neg = -jnp.inf
    big = float(LANES)
    is_grp = lane < N_EXP_GROUPS
    gl = jnp.where(is_grp, logits, neg)
    gmax = jnp.max(gl, axis=-1, keepdims=True)
    gsum = jnp.sum(jnp.exp(gl - gmax), axis=-1, keepdims=True)
    g_idx = jnp.min(jnp.where(jnp.logical_and(is_grp, gl == gmax), lane, big), axis=-1, keepdims=True)
    g_w = 1.0 / gsum
    e_lo = N_EXP_GROUPS + g_idx * EXP_PER_GROUP
    in_grp = jnp.logical_and(lane >= e_lo, lane < e_lo + EXP_PER_GROUP)
    el = jnp.where(in_grp, logits, neg)
    emax = jnp.max(el, axis=-1, keepdims=True)
    ee = jnp.exp(el - emax)
    esum = jnp.sum(ee, axis=-1, keepdims=True)
    i0 = jnp.min(jnp.where(jnp.logical_and(in_grp, el == emax), lane, big), axis=-1, keepdims=True)
    el2 = jnp.where(lane == i0, neg, el)
    emax2 = jnp.max(el2, axis=-1, keepdims=True)
    i1 = jnp.min(jnp.where(jnp.logical_and(in_grp, el2 == emax2), lane, big), axis=-1, keepdims=True)
    w0 = 1.0 / esum
    w1 = jnp.exp(emax2 - emax) / esum
    wsum = w0 + w1
    gate0 = g_w * w0 / wsum
    gate1 = g_w * w1 / wsum
    id0 = i0 - N_EXP_GROUPS
    id1 = i1 - N_EXP_GROUPS
    route = jnp.where(lane == 0, id0, jnp.where(lane == 1, id1, jnp.where(lane == 2, gate0,
                      jnp.where(lane == 3, gate1, 0.0))))
    route_ref[...] = route


def _merge(ya, yb, ym, x, w_o, ln_g, ln_b, w_r_hi, w_r_lo, b_r, *, layer, tm):
    t = x.shape[0]
    row = lambda i: (i, 0)
    lay = lambda i: (layer, 0, 0)
    return pl.pallas_call(
        _merge_kernel,
        grid=(t // tm,),
        in_specs=[pl.BlockSpec((tm, D_MODEL), row)] * 4 + [
            pl.BlockSpec((None, D_MODEL, D_MODEL), lay),
            pl.BlockSpec((None, 1, D_MODEL), lay), pl.BlockSpec((None, 1, D_MODEL), lay),
            pl.BlockSpec((None, D_MODEL, LANES), lay), pl.BlockSpec((None, D_MODEL, LANES), lay),
            pl.BlockSpec((None, 1, LANES), lay)],
        out_specs=[pl.BlockSpec((tm * (D_MODEL // LANES), LANES), row), pl.BlockSpec((tm, LANES), row)],
        out_shape=[jax.ShapeDtypeStruct((t * (D_MODEL // LANES), LANES), F32),
                   jax.ShapeDtypeStruct((t, LANES), F32)],
        compiler_params=_params("parallel"),
        name="merge_ln_router",
    )(ya, yb, ym, x, w_o, ln_g, ln_b, w_r_hi, w_r_lo, b_r)


def _moe_kernel(bexp_ref, nused_ref, starts_ref, pstarts_ref, counts_ref, bord_ref, elist_ref, nexp_ref,
                tok_ref, row_ref, gate_ref, x_ref, win_hbm, wout_hbm, g_ref, b_ref, o_ref,
                xg_ref, xg2_ref, eo_ref, eo2_ref, y2_ref, win_buf, wout_buf, wsem, *, nblk, tc, layer):
    c = pl.program_id(0)
    i = pl.program_id(1)
    nslab = D_MODEL // LANES

    def tile(t):
        return pl.ds(pl.multiple_of(t * nslab, nslab), nslab)

    def weight_copies(expert, slot):
        return (pltpu.make_async_copy(win_hbm.at[layer, expert], win_buf.at[slot], wsem.at[slot, 0]),
                pltpu.make_async_copy(wout_hbm.at[layer, expert], wout_buf.at[slot], wsem.at[slot, 1]))

    def fetch(ordinal):
        @pl.when(ordinal < nexp_ref[c])
        def _():
            for cp in weight_copies(elist_ref[c * N_EXPERTS + ordinal], ordinal % MOE_WEIGHT_SLOTS):
                cp.start()

    @pl.when(i < nused_ref[c])
    def _():
        ordinal = bord_ref[c * nblk + i]
        slot = ordinal % MOE_WEIGHT_SLOTS
        first_block = jnp.logical_or(i == 0, bord_ref[jnp.maximum(c * nblk + i - 1, 0)] != ordinal)

        @pl.when(i == 0)
        def _():
            for ahead in range(MOE_WEIGHT_SLOTS - 1):
                fetch(ahead)

        @pl.when(first_block)
        def _():
            fetch(ordinal + MOE_WEIGHT_SLOTS - 1)
            for cp in weight_copies(0, slot):
                cp.wait()

        def block_pos(blk):
            tbl = c * N_EXPERTS + bexp_ref[c * nblk + blk]
            off = blk * MOE_BLOCK - pstarts_ref[tbl]
            return starts_ref[tbl] + off, counts_ref[tbl] - off

        def gather(first_pos, dst_ref, r_lo=0, r_hi=MOE_BLOCK):
            for r in range(r_lo, r_hi):
                dst_ref[r * nslab:(r + 1) * nslab, :] = x_ref[tile(tok_ref[0, 0, first_pos + r]), :]

        def scatter(first_pos, nreal, src_ref, r_lo=0, r_hi=MOE_BLOCK):
            for r in range(r_lo, r_hi):
                row = jnp.where(r < nreal, row_ref[0, 0, first_pos + r], EXP_TOPK * tc)
                y2_ref[tile(row), :] = gate_ref[0, 0, first_pos + r] * src_ref[r * nslab:(r + 1) * nslab, :]

        last_used = nused_ref[c] - 1
        pos0, nvalid = block_pos(i)
        next_pos0, _ = block_pos(jnp.minimum(i + 1, last_used))
        prev_pos0, prev_nvalid = block_pos(jnp.maximum(i - 1, 0))
        prev_nvalid = jnp.where(i > 0, prev_nvalid, 0)

        @pl.when(i == 0)
        def _():
            gather(pos0, xg_ref)

            @pl.when(c == 0)
            def _():
                eo2_ref[...] = jnp.zeros_like(eo2_ref)

        def run(cur_ref, nxt_ref, eo_cur, eo_prev):
            xb = _token_rows(cur_ref, 0, MOE_BLOCK).astype(BF16)
            pieces = []
            npiece = 4
            cols = 2 * EXP_FF // npiece
            rows_per = MOE_BLOCK // npiece
            for n in range(npiece):
                pieces.append(_dot(xb, win_buf[slot, :, n * cols:(n + 1) * cols]))
                gather(next_pos0, nxt_ref, n * rows_per, (n + 1) * rows_per)
                scatter(prev_pos0, prev_nvalid, eo_prev, n * rows_per, (n + 1) * rows_per)
            gu = jnp.concatenate(pieces, axis=-1)
            hid = _silu(gu[:, :EXP_FF]) * gu[:, EXP_FF:]
            eo = _dot(hid.astype(BF16), wout_buf[slot])
            for j in range(nslab):
                eo_cur[pl.ds(j, MOE_BLOCK, stride=nslab), :] = eo[:, j * LANES:(j + 1) * LANES]

            @pl.when(i == last_used)
            def _():
                scatter(pos0, nvalid, eo_cur)

        @pl.when(i % 2 == 0)
        def _():
            run(xg_ref, xg2_ref, eo_ref, eo2_ref)

        @pl.when(i % 2 == 1)
        def _():
            run(xg2_ref, xg_ref, eo2_ref, eo_ref)

    @pl.when(i == nblk - 1)
    def _():
        step = min(MOE_FINISH_ROWS, tc)

        def finish(blk, carry):
            t0 = pl.multiple_of(blk * step, step)
            ffn = _token_rows(y2_ref, t0, step) + _token_rows(y2_ref, tc + t0, step)
            o_ref[pl.ds(t0, step), :] = _layer_norm(ALPHA * _token_rows(x_ref, t0, step) + ffn,
                                                    g_ref[...], b_ref[...])
            return carry

        lax.fori_loop(0, tc // step, finish, 0)


def _token_rows(ref, t0, n):
    nslab = D_MODEL // LANES
    return jnp.concatenate([ref[pl.ds(t0 * nslab + j, n, stride=nslab), :] for j in range(nslab)], axis=-1)


def _moe(x1_tiles, tables, slot_arrays, w_in, w_out, ln_g, ln_b, *, layer, tc, nblk):
    nslab = D_MODEL // LANES
    t = x1_tiles.shape[0] // nslab
    nchunks = t // tc
    s = tc * EXP_TOPK
    kern = functools.partial(_moe_kernel, nblk=nblk, tc=tc, layer=layer)
    smem_row = lambda c, i, *_: (c, 0, 0)
    lay = lambda c, i, *_: (layer, 0, 0)
    grid_spec = pltpu.PrefetchScalarGridSpec(
        num_scalar_prefetch=len(tables),
        grid=(nchunks, nblk),
        in_specs=[pl.BlockSpec((1, 1, s + MOE_BLOCK), smem_row, memory_space=pltpu.SMEM)] * 3 + [
                  pl.BlockSpec((tc * nslab, LANES), lambda c, i, *_: (c, 0), pipeline_mode=pl.Buffered(1)),
                  pl.BlockSpec(memory_space=pl.ANY),
                  pl.BlockSpec(memory_space=pl.ANY),
                  pl.BlockSpec((None, 1, D_MODEL), lay),
                  pl.BlockSpec((None, 1, D_MODEL), lay)],
        out_specs=pl.BlockSpec((tc, D_MODEL), lambda c, i, *_: (c, 0)),
        scratch_shapes=[pltpu.VMEM((MOE_BLOCK * nslab, LANES), F32)] * 4 + [
                        pltpu.VMEM(((EXP_TOPK * tc + 1) * nslab, LANES), F32),
                        pltpu.VMEM((MOE_WEIGHT_SLOTS, D_MODEL, 2 * EXP_FF), BF16),
                        pltpu.VMEM((MOE_WEIGHT_SLOTS, EXP_FF, D_MODEL), BF16),
                        pltpu.SemaphoreType.DMA((MOE_WEIGHT_SLOTS, 2))],
    )
    return pl.pallas_call(
        kern,
        grid_spec=grid_spec,
        out_shape=jax.ShapeDtypeStruct((t, D_MODEL), F32),
        compiler_params=_params("arbitrary", "arbitrary"),
        name="moe_experts",
    )(*tables, *[a[:, None, :] for a in slot_arrays], x1_tiles, w_in, w_out, ln_g, ln_b)


def _dispatch(route, *, tc, nblk):
    t = route.shape[0]
    nchunks = t // tc
    s = tc * EXP_TOPK
    ids = route[:, 0:EXP_TOPK].astype(jnp.int32).reshape(nchunks, s)
    gates = route[:, EXP_TOPK:2 * EXP_TOPK].reshape(nchunks, s)
    slot = jnp.broadcast_to(jnp.arange(s, dtype=jnp.int32)[None, :], (nchunks, s))
    _, order, gate_sorted = lax.sort((ids, slot, gates), dimension=1, is_stable=True, num_keys=1)
    tok_sorted = order // EXP_TOPK
    row_sorted = (order % EXP_TOPK) * tc + tok_sorted
    counts =jnp.sum((ids[:, :, None] == jnp.arange(N_EXPERTS, dtype=jnp.int32)).astype(jnp.int32), axis=1)
    starts = jnp.cumsum(counts, axis=-1) - counts
    padded = (counts + MOE_BLOCK - 1) // MOE_BLOCK * MOE_BLOCK
    pends = jnp.cumsum(padded, axis=-1)
    pstarts = pends - padded
    blk_start = jnp.arange(nblk, dtype=jnp.int32) * MOE_BLOCK
    bexp = jnp.sum((pends[:, None, :] <= blk_start[None, :, None]).astype(jnp.int32), axis=-1)
    bexp = jnp.minimum(bexp, N_EXPERTS - 1).reshape(-1)
    nused = pends[:, -1] // MOE_BLOCK
    nonempty = counts > 0
    nexp = jnp.sum(nonempty.astype(jnp.int32), axis=-1)
    ordinal_of_expert = jnp.cumsum(nonempty.astype(jnp.int32), axis=-1) - 1
    elist = jnp.argsort(jnp.logical_not(nonempty), axis=-1, stable=True).astype(jnp.int32)
    bexp_2d = bexp.reshape(nchunks, nblk)
    bord = jnp.sum(jnp.where(bexp_2d[:, :, None] == jnp.arange(N_EXPERTS, dtype=jnp.int32),
                             ordinal_of_expert[:, None, :], 0), axis=-1).astype(jnp.int32)
    tables = (bexp, nused, starts.reshape(-1), pstarts.reshape(-1), counts.reshape(-1), bord.reshape(-1),
              elist.reshape(-1), nexp)
    pad = ((0, 0), (0, MOE_BLOCK))
    return tables, (jnp.pad(tok_sorted, pad), jnp.pad(row_sorted, pad), jnp.pad(gate_sorted, pad))


def _prep_weights(w_in, w_sconv_out, w_ssd_out, w_mem_k, w_mem_v, w_mem_out, w_o, w_exp_in, w_exp_out,
                  w_router_grp, b_router_grp, w_router_exp, b_router_exp, ssd_dt_bias, ssd_a_log, ssd_d):
    o_b, o_z, o_xbc, o_dt, o_q, o_g = 0, 3072, 5120, 8192, 8224, 9248
    w_main = jnp.concatenate([w_in[:, :, o_xbc:o_dt], w_in[:, :, o_b:o_z], w_in[:, :, o_z:o_xbc],
                              w_in[:, :, o_q:o_g], w_in[:, :, o_g:]], axis=-1).astype(BF16)
    pad_h = LANES - SSD_HEADS
    w_dt = jnp.pad(w_in[:, :, o_dt:o_q], ((0, 0), (0, 0), (0, pad_h))).astype(BF16)
    pad_r = LANES - N_EXP_GROUPS - N_EXPERTS
    w_r = jnp.pad(jnp.concatenate([w_router_grp, w_router_exp], axis=-1), ((0, 0), (0, 0), (0, pad_r)))
    w_r_hi = w_r.astype(BF16)
    w_r_lo = (w_r - w_r_hi.astype(F32)).astype(BF16)
    b_r = jnp.pad(jnp.concatenate([b_router_grp, b_router_exp], axis=-1), ((0, 0), (0, pad_r)))[:, None, :]
    head_of_col = jnp.arange(SSD_INNER, dtype=jnp.int32) // SSD_HEADDIM
    expand = (jnp.arange(LANES, dtype=jnp.int32)[:, None] == head_of_col[None, :]).astype(BF16)
    return dict(
        w_main=w_main, w_dt=w_dt, w_r_hi=w_r_hi, w_r_lo=w_r_lo, b_r=b_r, expand=expand,
        w_sconv_out=w_sconv_out.astype(BF16), w_ssd_out=w_ssd_out.astype(BF16),
        w_mem_kv=jnp.concatenate([w_mem_k, w_mem_v], axis=-1).astype(BF16),
        w_mem_out=w_mem_out.astype(BF16), w_o=w_o.astype(BF16),
        w_exp_in=w_exp_in.astype(BF16), w_exp_out=w_exp_out.astype(BF16),
        dt_bias=jnp.pad(ssd_dt_bias, ((0, 0), (0, pad_h)))[:, None, :],
        a_log=jnp.pad(ssd_a_log, ((0, 0), (0, pad_h)))[:, None, :],
        d_exp=jnp.repeat(ssd_d, SSD_HEADDIM, axis=-1)[:, None, :],
    )


def _moe_blocks(tc):
    return (tc * EXP_TOPK + N_EXPERTS * (MOE_BLOCK - 1) + MOE_BLOCK - 1) // MOE_BLOCK


def _decoder_layer(l, x, mem_k, mem_v, states_in, prev_layer, states_out, wp, p, cfg):
    nb, ls = cfg["nb"], cfg["ls"]
    sc_in, cv_in, h_in = states_in
    sc_out, cv_out, h_out = states_out
    proj = _matmul(x, wp["w_main"], l, cfg["tm_proj"], 2048)
    dtr = _matmul(x, wp["w_dt"], l, cfg["tm_proj"], LANES)
    ya, sc_new = _branch_a(proj, sc_in, prev_layer, sc_out, p["w_sconv"], wp["w_sconv_out"],
                           layer=l, nb=nb, ls=ls, nseq=cfg["a_nseq"], lt=cfg["a_lt"], valid=cfg["a_valid"])
    yb, cv_new, h_new = _branch_b(proj, dtr, cv_in, h_in, prev_layer, cv_out, h_out, p["w_ssd_conv"],
                                  p["b_ssd_conv"], wp["dt_bias"], wp["a_log"], wp["d_exp"], p["ssd_norm_w"],
                                  wp["expand"], wp["w_ssd_out"],
                                  layer=l, nb=nb, ls=ls, nseq=cfg["b_nseq"], lt=cfg["b_lt"], q=cfg["b_q"],
                                  valid=cfg["b_valid"])
    ym = _branch_m(proj, mem_k, mem_v, wp["w_mem_out"], layer=l, nb=nb, ls=ls, nseq=cfg["m_nseq"],
                   lt=cfg["m_lt"])
    x1, route = _merge(ya, yb, ym, x, wp["w_o"], p["ln1_g"], p["ln1_b"], wp["w_r_hi"], wp["w_r_lo"], wp["b_r"],
                       layer=l, tm=cfg["tm_merge"])
    tc = cfg["moe_tc"]
    nblk = _moe_blocks(tc)
    tables, slot_arrays = _dispatch(route, tc=tc, nblk=nblk)
    x2 = _moe(x1, tables, slot_arrays, wp["w_exp_in"], wp["w_exp_out"], p["ln2_g"], p["ln2_b"],
              layer=l, tc=tc, nblk=nblk)
    return x2, (sc_new, cv_new, h_new)


PROMPT_CFG = dict(tm_proj=1024, a_nseq=1, a_lt=512, b_nseq=1, b_lt=256, b_q=SSD_CHUNK, m_nseq=1, m_lt=512,
                  tm_merge=512, moe_tc=2048)
SAMPLE_PAD = SUBLANES


def _zero_states(nb):
    return (jnp.zeros((DEPTH, nb, SC_CONV_W - 1, SC_DIM), F32),
            jnp.zeros((DEPTH, nb, SSD_CONV_W - 1, SSD_CONV_DIM), F32),
            jnp.zeros((DEPTH, nb, SSD_HEADS, SSD_HEADDIM, SSD_STATE), F32))


def kernel(x_prompt, x_sample, mem_prompt, cache_mem_k, cache_mem_v, state_sconv, state_ssd_conv, state_ssd, w_in, w_sconv, w_sconv_out, w_ssd_conv, b_ssd_conv, ssd_dt_bias, ssd_a_log, ssd_d, ssd_norm_w, w_ssd_out, w_mem_k, w_mem_v, w_mem_out, w_o, ln1_g, ln1_b, w_router_grp, b_router_grp, w_router_exp, b_router_exp, w_exp_in, w_exp_out, ln2_g, ln2_b):
    bp, seq, _ = x_prompt.shape
    bs, dec_seq, _ = x_sample.shape
    wp = _prep_weights(w_in, w_sconv_out, w_ssd_out, w_mem_k, w_mem_v, w_mem_out, w_o, w_exp_in, w_exp_out,
                       w_router_grp, b_router_grp, w_router_exp, b_router_exp, ssd_dt_bias, ssd_a_log, ssd_d)
    p = dict(w_sconv=w_sconv, w_ssd_conv=w_ssd_conv, b_ssd_conv=b_ssd_conv[:, None, :],
             ssd_norm_w=ssd_norm_w[:, None, :], ln1_g=ln1_g[:, None, :], ln1_b=ln1_b[:, None, :],
             ln2_g=ln2_g[:, None, :], ln2_b=ln2_b[:, None, :])

    cfg_p = dict(PROMPT_CFG, nb=bp, ls=seq, a_valid=PROMPT_CFG["a_lt"], b_valid=PROMPT_CFG["b_lt"])
    mem_rows = mem_prompt.reshape(bp * MEM_LEN, D_MODEL)
    zeros_in = (jnp.zeros((1, bp, SC_CONV_W - 1, SC_DIM), F32),
                jnp.zeros((1, bp, SSD_CONV_W - 1, SSD_CONV_DIM), F32),
                jnp.zeros((1, bp, SSD_HEADS, SSD_HEADDIM, SSD_STATE), F32))
    yp = x_prompt.reshape(bp * seq, D_MODEL)
    mk_p = jnp.zeros((DEPTH, bp, MEM_LEN, MEM_HEADS, MEM_HEAD_DIM), F32)
    mv_p = jnp.zeros((DEPTH, bp, MEM_LEN, MEM_HEADS, MEM_HEAD_DIM), F32)
    st_p = _zero_states(bp)
    for l in range(DEPTH):
        mk_p = _memkv(mem_rows, wp["w_mem_kv"], l, 0, mk_p, nb=bp, nbt=4)
        mv_p = _memkv(mem_rows, wp["w_mem_kv"], l, 1, mv_p, nb=bp, nbt=4)
        yp, st_p = _decoder_layer(l, yp, mk_p, mv_p, zeros_in, 0, st_p, wp, p, cfg_p)

    ls = SAMPLE_PAD
    cfg_s = dict(tm_proj=bs * ls, nb=bs, ls=ls, a_nseq=16, a_lt=ls, a_valid=dec_seq, b_nseq=4, b_lt=ls, b_q=ls,
                 b_valid=dec_seq, m_nseq=2, m_lt=ls, tm_merge=512, moe_tc=bs * ls)
    ys = jnp.pad(x_sample, ((0, 0), (0, ls - dec_seq), (0, 0))).reshape(bs * ls, D_MODEL)
    st_s = _zero_states(bs)
    for l in range(DEPTH):
        ys, st_s = _decoder_layer(l, ys, cache_mem_k, cache_mem_v, (state_sconv, state_ssd_conv, state_ssd), l,
                                  st_s, wp, p, cfg_s)

    y_prompt = yp.reshape(bp, seq, D_MODEL)
    y_sample = ys.reshape(bs, ls, D_MODEL)[:, :dec_seq]
    return (y_prompt, y_sample, mk_p, mv_p, st_p[0], st_p[1], st_p[2], st_s[0], st_s[1], st_s[2])
```

```python
import functools

import jax
import jax.numpy as jnp
from jax import lax
from jax.experimental import pallas as pl
from jax.experimental.pallas import tpu as pltpu

F32 = jnp.float32
BF16 = jnp.bfloat16

D_MODEL = 1024
DEPTH = 4
SC_DIM = D_MODEL
SC_CONV_W = 3
SSD_INNER = 2 * D_MODEL
SSD_HEADDIM = 64
SSD_HEADS = SSD_INNER // SSD_HEADDIM
SSD_GROUPS = 4
SSD_STATE = 128
SSD_CONV_W = 4
SSD_BC = SSD_GROUPS * SSD_STATE
SSD_CONV_DIM = SSD_INNER + 2 * SSD_BC
SSD_CHUNK = 128
HEADS_PER_GROUP = SSD_HEADS // SSD_GROUPS
GROUP_INNER = SSD_INNER // SSD_GROUPS
MEM_LEN = 256
MEM_HEADS = 4
MEM_HEAD_DIM = D_MODEL // MEM_HEADS
N_EXP_GROUPS = 4
EXP_PER_GROUP = 8
N_EXPERTS = N_EXP_GROUPS * EXP_PER_GROUP
EXP_TOPK = 2
EXP_FF = 512
MOE_BLOCK = 128
MOE_FINISH_ROWS = 256
MOE_WEIGHT_SLOTS = 3
ALPHA = (2 * DEPTH) ** 0.25
LN_EPS = 1e-5
RMS_EPS = 1e-6

LANES = 128
SUBLANES = 8
VMEM_LIMIT = 56 * 1024 * 1024

COL_XBC = 0
COL_BCX = SSD_CONV_DIM
COL_Z = COL_BCX + 3 * SC_DIM
COL_Q = COL_Z + SSD_INNER
COL_G = COL_Q + D_MODEL
PROJ_COLS = COL_G + 3 * D_MODEL


def _dot(a, b):
    return jnp.dot(a, b, preferred_element_type=F32)


def _split_dot(a, b, terms, dims=(((1,), (0,)), ((), ()))):
    acc = None
    rest = a
    for t in range(terms):
        piece = rest.astype(BF16)
        part = lax.dot_general(piece, b, dims, preferred_element_type=F32)
        acc = part if acc is None else acc + part
        if t + 1 < terms:
            rest = rest - piece.astype(F32)
    return acc


def _sigmoid(x):
    return 1.0 / (1.0 + jnp.exp(-x))


def _silu(x):
    return x * _sigmoid(x)


def _layer_norm(x, g, b):
    mu = jnp.mean(x, axis=-1, keepdims=True)
    xc = x - mu
    var = jnp.mean(xc * xc, axis=-1, keepdims=True)
    return xc * lax.rsqrt(var + LN_EPS) * g + b


def _params(*sem):
    return pltpu.CompilerParams(dimension_semantics=sem, vmem_limit_bytes=VMEM_LIMIT)


def _skip_alias_refs(kernel_fn, n_in, n_alias):
    def wrapped(*refs):
        return kernel_fn(*refs[:n_in], *refs[n_in + n_alias:])
    return wrapped


def _alias_args(prevs, n_in, first_out):
    prevs = [p for p in prevs if p is not None]
    specs = [pl.BlockSpec(memory_space=pl.ANY)] * len(prevs)
    aliases = {n_in + k: first_out + k for k in range(len(prevs))}
    return prevs, specs, aliases


def _mm_kernel(x_ref, w_ref, o_ref):
    o_ref[...] = _dot(x_ref[...].astype(BF16), w_ref[...])


def _matmul(x, w, layer, tm, tn):
    m, k = x.shape
    n = w.shape[2]
    tm = min(tm, m)
    tn = min(tn, n)
    return pl.pallas_call(
        _mm_kernel,
        grid=(m // tm, n // tn),
        in_specs=[pl.BlockSpec((tm, k), lambda i, j: (i, 0)),
                  pl.BlockSpec((None, k, tn), lambda i, j: (layer, 0, j))],
        out_specs=pl.BlockSpec((tm, tn), lambda i, j: (i, j)),
        out_shape=jax.ShapeDtypeStruct((m, n), F32),
        compiler_params=_params("parallel", "arbitrary"),
        name="matmul",
    )(x, w)


def _memkv_kernel(x_ref, w_ref, o_ref, *, nbt):
    res = _dot(x_ref[...].astype(BF16), w_ref[...])
    for h in range(MEM_HEADS):
        o_ref[:, :, h, :] = res[:, h * MEM_HEAD_DIM:(h + 1) * MEM_HEAD_DIM].reshape(nbt, MEM_LEN, MEM_HEAD_DIM)


def _memkv(mem_rows, w_kv, layer, which, stacked_prev, *, nb, nbt):
    prevs, pspecs, aliases = _alias_args([stacked_prev], 2, 0)
    kern = _skip_alias_refs(functools.partial(_memkv_kernel, nbt=nbt), 2, len(prevs))
    return pl.pallas_call(
        kern,
        grid=(nb // nbt,),
        in_specs=[pl.BlockSpec((nbt * MEM_LEN, D_MODEL), lambda i: (i, 0)),
                  pl.BlockSpec((None, D_MODEL, D_MODEL), lambda i: (layer, 0, which))] + pspecs,
        out_specs=pl.BlockSpec((None, nbt, MEM_LEN, MEM_HEADS, MEM_HEAD_DIM), lambda i: (layer, i, 0, 0, 0)),
        out_shape=jax.ShapeDtypeStruct((DEPTH, nb, MEM_LEN, MEM_HEADS, MEM_HEAD_DIM), F32),
        input_output_aliases=aliases,
        compiler_params=_params("parallel"),
        name="mem_kv",
    )(mem_rows, w_kv, *prevs)


def _shift_rows(u, tail, d):
    nseq, lt, c_dim = u.shape
    groups = lt // SUBLANES
    row = lax.broadcasted_iota(jnp.int32, (SUBLANES, c_dim), 0)
    outs = []
    for s in range(nseq):
        rot = pltpu.roll(u[s].reshape(groups, SUBLANES, c_dim), d, axis=1)
        before = pltpu.roll(tail[s], d, axis=0)[None]
        if groups > 1:
            before = jnp.concatenate([before, rot[:groups - 1]], axis=0)
        outs.append(jnp.where(row < d, before, rot).reshape(lt, c_dim))
    return jnp.stack(outs, axis=0)


def _causal_conv(u, prev_ref, w, st_ref, ext_ref, *, valid):
    nseq, lt, _ = u.shape
    width = w.shape[0]
    halo = width - 1

    @pl.when(pl.program_id(1) == 0)
    def _():
        ext_ref[:, 0:SUBLANES, :] = jnp.zeros_like(ext_ref[:, 0:SUBLANES, :])
        ext_ref[:, SUBLANES - halo:SUBLANES, :] = prev_ref[...]

    tail = ext_ref[:, 0:SUBLANES, :]
    last8 = u[:, lt - SUBLANES:lt, :]
    ext_ref[:, SUBLANES:2 * SUBLANES, :] = last8
    out = _shift_rows(u, tail, halo) * w[0:1, :]
    for k in range(1, halo):
        out = out + _shift_rows(u, tail, halo - k) * w[k:k + 1, :]
    out = out + u * w[halo:width, :]
    end = 2 * SUBLANES - (lt - valid)
    st_ref[...] = ext_ref[:, end - halo:end, :]
    ext_ref[:, 0:SUBLANES, :] = last8
    return out


def _branch_a_kernel(bcx_ref, g_ref, prev_ref, wc_ref, wout_ref, o_ref, st_ref, ext_ref, *, nseq, lt, valid):
    c_dim = SC_DIM
    rows = nseq * lt
    bcx = bcx_ref[...]
    scb = bcx[:, :c_dim]
    u = (bcx[:, c_dim:2 * c_dim] * bcx[:, 2 * c_dim:]).reshape(nseq, lt, c_dim)
    v = _causal_conv(u, prev_ref, wc_ref[...], st_ref, ext_ref, valid=valid)
    ya = _dot((scb * v.reshape(rows, c_dim)).astype(BF16), wout_ref[...])
    o_ref[...] = _sigmoid(g_ref[...]) * ya


def _branch_a(proj, prev, prev_layer, st_prev, w_conv, w_out, *, layer, nb, ls, nseq, lt, valid):
    nj = ls // lt
    rows = nseq * lt
    n_in = 5
    prevs, pspecs, aliases = _alias_args([st_prev], n_in, 1)
    kern = _skip_alias_refs(functools.partial(_branch_a_kernel, nseq=nseq, lt=lt, valid=valid), n_in, len(prevs))
    return pl.pallas_call(
        kern,
        grid=(nb // nseq, nj),
        in_specs=[pl.BlockSpec((rows, 3 * SC_DIM), lambda i, j: (i * nj + j, COL_BCX // (3 * SC_DIM))),
                  pl.BlockSpec((rows, D_MODEL), lambda i, j: (i * nj + j, COL_G // D_MODEL)),
                  pl.BlockSpec((None, nseq, SC_CONV_W - 1, SC_DIM), lambda i, j: (prev_layer, i, 0, 0)),
                  pl.BlockSpec((None, SC_CONV_W, SC_DIM), lambda i, j: (layer, 0, 0)),
                  pl.BlockSpec((None, SC_DIM, D_MODEL), lambda i, j: (layer, 0, 0))] + pspecs,
        out_specs=[pl.BlockSpec((rows, D_MODEL), lambda i, j: (i * nj + j, 0)),
                   pl.BlockSpec((None, nseq, SC_CONV_W - 1, SC_DIM), lambda i, j: (layer, i, 0, 0))],
        out_shape=[jax.ShapeDtypeStruct((nb * ls, D_MODEL), F32),
                   jax.ShapeDtypeStruct((DEPTH, nb, SC_CONV_W - 1, SC_DIM), F32)],
        scratch_shapes=[pltpu.VMEM((nseq, 2 * SUBLANES, SC_DIM), F32)],
        input_output_aliases=aliases,
        compiler_params=_params("arbitrary", "arbitrary"),
        name="branch_a",
    )(proj, proj, prev, w_conv, w_out, *prevs)


def _load_head(kv_ref, h):
    slabs = [kv_ref[:, :, h, c * LANES:(c + 1) * LANES] for c in range(MEM_HEAD_DIM // LANES)]
    return jnp.concatenate(slabs, axis=-1).astype(BF16)


def _branch_m_kernel(q_ref, g_ref, k_ref, v_ref, wout_ref, o_ref, *, nseq, lt):
    rows = nseq * lt
    scale = MEM_HEAD_DIM ** -0.5
    qb = q_ref[...].astype(BF16).reshape(nseq, lt, D_MODEL)
    outs = []
    for h in range(MEM_HEADS):
        sl = slice(h * MEM_HEAD_DIM, (h + 1) * MEM_HEAD_DIM)
        kh = _load_head(k_ref, h)
        vh = _load_head(v_ref, h)
        s = jnp.einsum("sld,smd->slm", qb[:, :, sl], kh, preferred_element_type=F32) * scale
        s = s - jnp.max(s, axis=-1, keepdims=True)
        p = jnp.exp(s)
        p = p / jnp.sum(p, axis=-1, keepdims=True)
        outs.append(jnp.einsum("slm,smd->sld", p.astype(BF16), vh, preferred_element_type=F32))
    o = jnp.concatenate(outs, axis=-1).reshape(rows, D_MODEL)
    o_ref[...] = _sigmoid(g_ref[...]) * _dot(o.astype(BF16), wout_ref[...])


def _branch_m(proj, mem_k, mem_v, w_out, *, layer, nb, ls, nseq, lt):
    nj = ls // lt
    rows = nseq * lt
    kern = functools.partial(_branch_m_kernel, nseq=nseq, lt=lt)
    kv_spec = pl.BlockSpec((None, nseq, MEM_LEN, MEM_HEADS, MEM_HEAD_DIM), lambda i, j: (layer, i, 0, 0, 0))
    return pl.pallas_call(
        kern,
        grid=(nb // nseq, nj),
        in_specs=[pl.BlockSpec((rows, D_MODEL), lambda i, j: (i * nj + j, COL_Q // D_MODEL)),
                  pl.BlockSpec((rows, D_MODEL), lambda i, j: (i * nj + j, COL_G // D_MODEL + 2)),
                  kv_spec, kv_spec,
                  pl.BlockSpec((None, D_MODEL, D_MODEL), lambda i, j: (layer, 0, 0))],
        out_specs=pl.BlockSpec((rows, D_MODEL), lambda i, j: (i * nj + j, 0)),
        out_shape=jax.ShapeDtypeStruct((nb * ls, D_MODEL), F32),
        compiler_params=_params("parallel", "arbitrary"),
        name="branch_m",
    )(proj, proj, mem_k, mem_v, w_out)


def _softplus(x):
    return jnp.maximum(x, 0.0) + jnp.log1p(jnp.exp(-jnp.abs(x)))


def _ssd_chunk(s, c, xc_ref, z_ref, dt_ref, h_ref, yn_ref, a_row, d_row, nw_row, expand, *, lt, q):
    r0 = pl.multiple_of(c * q, q)
    row0 = pl.multiple_of(s * lt + c * q, q)
    xs = xc_ref[s, pl.ds(r0, q), 0:SSD_INNER]
    bm = xc_ref[s, pl.ds(r0, q), SSD_INNER:SSD_INNER + SSD_BC].astype(BF16)
    cm = xc_ref[s, pl.ds(r0, q), SSD_INNER + SSD_BC:SSD_CONV_DIM].astype(BF16)
    dt = dt_ref[pl.ds(row0, q), :]
    da = dt * a_row
    ti = lax.broadcasted_iota(jnp.int32, (q, q), 0)
    si = lax.broadcasted_iota(jnp.int32, (q, q), 1)
    causal = si <= ti
    tri = jnp.where(causal, 1.0, 0.0).astype(BF16)
    acs = _tri_cumsum(tri, da)
    acs_t = _split_dot(da, tri, 3, dims=(((0,), (1,)), ((), ())))
    acs_last = acs[q - 1:q, :]
    eacs = jnp.exp(acs)
    eacs_hi = eacs.astype(BF16).astype(F32)
    stack = jnp.concatenate([dt, jnp.exp(acs_last - acs), eacs_hi, eacs - eacs_hi], axis=0)
    stack_e = _dot(stack.astype(BF16), expand)
    dt_e, dte_e = stack_e[0:q], stack_e[q:2 * q]
    eacs_e = stack_e[2 * q:3 * q] + stack_e[3 * q:4 * q]
    xdt = xs * dt_e
    xdt_b = xdt.astype(BF16)
    xdte_b = (xdt * dte_e).astype(BF16)
    dec_full = jnp.broadcast_to(jnp.exp(acs_t[:, q - 1:q]), (LANES, SSD_STATE))
    lane = lax.broadcasted_iota(jnp.int32, (q, LANES), 1)
    lo_mask = lane < SSD_HEADDIM
    y_groups = []
    for g in range(SSD_GROUPS):
        gs = slice(g * SSD_STATE, (g + 1) * SSD_STATE)
        gi = slice(g * GROUP_INNER, (g + 1) * GROUP_INNER)
        cb = lax.dot_general(cm[:, gs], bm[:, gs], (((1,), (1,)), ((), ())), preferred_element_type=F32)
        h0 = g * HEADS_PER_GROUP
        hg = h_ref[s, h0:h0 + HEADS_PER_GROUP].reshape(GROUP_INNER, SSD_STATE)
        y_int = lax.dot_general(cm[:, gs], hg.astype(BF16), (((1,), (1,)), ((), ())),
                                preferred_element_type=F32) * eacs_e[:, gi]
        upd = lax.dot_general(xdte_b[:, gi], bm[:, gs], (((0,), (0,)), ((), ())), preferred_element_type=F32)
        y_pairs = []
        for pr in range(HEADS_PER_GROUP // 2):
            col = g * GROUP_INNER + pr * LANES
            xpair = xdt_b[:, col:col + LANES]
            zero = jnp.zeros_like(xpair)
            masks = []
            for hh in range(2):
                h = h0 + 2 * pr + hh
                seg = acs[:, h:h + 1] - acs_t[h:h + 1, :]
                lm = jnp.exp(jnp.where(causal, seg, -jnp.inf))
                masks.append((cb * lm).astype(BF16))
            x_lo = jnp.where(lo_mask, xpair, zero)
            x_hi = jnp.where(lo_mask, zero, xpair)
            if q % LANES == 0:
                y_pairs.append(_dot(jnp.concatenate(masks, axis=1), jnp.concatenate([x_lo, x_hi], axis=0)))
            else:
                y_pairs.append(_dot(masks[0], x_lo) + _dot(masks[1], x_hi))
        y_groups.append(jnp.concatenate(y_pairs, axis=-1) + y_int)
        for hh in range(HEADS_PER_GROUP):
            h = h0 + hh
            h_ref[s, h] = h_ref[s, h] * dec_full[h:h + 1, :] + upd[hh * SSD_HEADDIM:(hh + 1) * SSD_HEADDIM, :]
    y = jnp.concatenate(y_groups, axis=-1) + d_row * xs
    yf = y * _silu(z_ref[pl.ds(row0, q), :])
    normed = []
    for g in range(SSD_GROUPS):
        yg = yf[:, g * GROUP_INNER:(g + 1) * GROUP_INNER]
        normed.append(yg * lax.rsqrt(jnp.mean(yg * yg, axis=-1, keepdims=True) + RMS_EPS))
    yn_ref[pl.ds(row0, q), :] = jnp.concatenate(normed, axis=-1) * nw_row


def _tri_cumsum(tri, da):
    acc = None
    rest = da
    for t in range(3):
        piece = rest.astype(BF16)
        part = _dot(tri, piece)
        acc = part if acc is None else acc + part
        if t < 2:
            rest = rest - piece.astype(F32)
    return acc


def _branch_b_kernel(xbc_ref, z_ref, x_ref, wdt_ref, g_ref, cprev_ref, h0_ref, wc_ref, bc_ref, dtb_ref, alog_ref,
                     d_ref, nw_ref, exp_ref, wout_ref, o_ref, cst_ref, h_ref, ext_ref, xc_ref, dt_ref, yn_ref,
                     *, nseq, lt, q, valid):
    rows = nseq * lt

    @pl.when(pl.program_id(1) == 0)
    def _():
        h_ref[...] = h0_ref[...]

    conv = _causal_conv(xbc_ref[...].reshape(nseq, lt, SSD_CONV_DIM), cprev_ref, wc_ref[...], cst_ref, ext_ref,
                        valid=valid)
    xc_ref[...] = _silu(conv + bc_ref[...])
    dt = _softplus(_dot(x_ref[...].astype(BF16), wdt_ref[...]) + dtb_ref[...])
    if valid < lt:
        t_in_seq = lax.broadcasted_iota(jnp.int32, (nseq, lt, LANES), 1).reshape(rows, LANES)
        dt = jnp.where(t_in_seq < valid, dt, 0.0)
    dt_ref[...] = dt
    a_row = -jnp.exp(alog_ref[...])
    d_row = d_ref[...]
    nw_row = nw_ref[...]
    expand = exp_ref[...]
    nchunk = lt // q

    def body(n, carry):
        _ssd_chunk(n // nchunk, n % nchunk, xc_ref, z_ref, dt_ref, h_ref, yn_ref, a_row, d_row, nw_row, expand,
                   lt=lt, q=q)
        return carry

    lax.fori_loop(0, nseq * nchunk, body, 0)
    o_ref[...] = _sigmoid(g_ref[...]) * _dot(yn_ref[...].astype(BF16), wout_ref[...])


def _branch_b(proj, x, w_dt, cprev, h0, prev_layer, cst_prev, hst_prev, w_conv, b_conv, dt_bias, a_log, d_exp, norm_w,
              expand, w_out, *, layer, nb, ls, nseq, lt, q, valid):
    nj = ls // lt
    rows = nseq * lt
    n_in = 15
    prevs, pspecs, aliases = _alias_args([cst_prev, hst_prev], n_in, 1)
    kern = _skip_alias_refs(functools.partial(_branch_b_kernel, nseq=nseq, lt=lt, q=q, valid=valid),
                            n_in, len(prevs))
    row_blk = lambda i, j: (i * nj + j, 0)
    lay3 = lambda i, j: (layer, 0, 0)
    return pl.pallas_call(
        kern,
        grid=(nb // nseq, nj),
        in_specs=[pl.BlockSpec((rows, SSD_CONV_DIM), lambda i, j: (i * nj + j, COL_XBC // SSD_CONV_DIM)),
                  pl.BlockSpec((rows, SSD_INNER), lambda i, j: (i * nj + j, COL_Z // SSD_INNER)),
                  pl.BlockSpec((rows, D_MODEL), row_blk),
                  pl.BlockSpec((None, D_MODEL, LANES), lay3),
                  pl.BlockSpec((rows, D_MODEL), lambda i, j: (i * nj + j, COL_G // D_MODEL + 1)),
                  pl.BlockSpec((None, nseq, SSD_CONV_W - 1, SSD_CONV_DIM), lambda i, j: (prev_layer, i, 0, 0)),
                  pl.BlockSpec((None, nseq, SSD_HEADS, SSD_HEADDIM, SSD_STATE),
                               lambda i, j: (prev_layer, i, 0, 0, 0)),
                  pl.BlockSpec((None, SSD_CONV_W, SSD_CONV_DIM), lay3),
                  pl.BlockSpec((None, 1, SSD_CONV_DIM), lay3),
                  pl.BlockSpec((None, 1, LANES), lay3),
                  pl.BlockSpec((None, 1, LANES), lay3),
                  pl.BlockSpec((None, 1, SSD_INNER), lay3),
                  pl.BlockSpec((None, 1, SSD_INNER), lay3),
                  pl.BlockSpec((LANES, SSD_INNER), lambda i, j: (0, 0)),
                  pl.BlockSpec((None, SSD_INNER, D_MODEL), lay3)] + pspecs,
        out_specs=[pl.BlockSpec((rows, D_MODEL), row_blk),
                   pl.BlockSpec((None, nseq, SSD_CONV_W - 1, SSD_CONV_DIM), lambda i, j: (layer, i, 0, 0)),
                   pl.BlockSpec((None, nseq, SSD_HEADS, SSD_HEADDIM, SSD_STATE),
                                lambda i, j: (layer, i, 0, 0, 0))],
        out_shape=[jax.ShapeDtypeStruct((nb * ls, D_MODEL), F32),
                   jax.ShapeDtypeStruct((DEPTH, nb, SSD_CONV_W - 1, SSD_CONV_DIM), F32),
                   jax.ShapeDtypeStruct((DEPTH, nb, SSD_HEADS, SSD_HEADDIM, SSD_STATE), F32)],
        scratch_shapes=[pltpu.VMEM((nseq, 2 * SUBLANES, SSD_CONV_DIM), F32),
                        pltpu.VMEM((nseq, lt, SSD_CONV_DIM), F32),
                        pltpu.VMEM((rows, LANES), F32),
                        pltpu.VMEM((rows, SSD_INNER), F32)],
        input_output_aliases=aliases,
        compiler_params=_params("arbitrary", "arbitrary"),
        name="branch_b",
    )(proj, proj, x, w_dt, proj, cprev, h0, w_conv, b_conv, dt_bias, a_log, d_exp, norm_w, expand, w_out, *prevs)


def _merge_kernel(ya_ref, yb_ref, ym_ref, x_ref, wo_ref, g_ref, b_ref, wrh_ref, wrl_ref, br_ref, x1_ref,
                  route_ref):
    merged = (ya_ref[...] + yb_ref[...]) + ym_ref[...]
    mix = _dot(merged.astype(BF16), wo_ref[...])
    x1 = _layer_norm(ALPHA * x_ref[...] + mix, g_ref[...], b_ref[...])
    nslab = D_MODEL // LANES
    for j in range(nslab):
        x1_ref[pl.ds(j, x1.shape[0], stride=nslab), :] = x1[:, j * LANES:(j + 1) * LANES]
    x_hi = x1.astype(BF16)
    x_lo = (x1 - x_hi.astype(F32)).astype(BF16)
    w_hi = wrh_ref[...]
    logits = (_dot(x_hi, w_hi) + (_dot(x_lo, w_hi) + _dot(x_hi, wrl_ref[...]))) + br_ref[...]
    lane = lax.broadcasted_iota(jnp.int32, logits.shape, 1).astype(F32)
    neg = -jnp.inf
    big = float(LANES)
    is_grp = lane < N_EXP_GROUPS
    gl = jnp.where(is_grp, logits, neg)
    gmax = jnp.max(gl, axis=-1, keepdims=True)
    gsum = jnp.sum(jnp.exp(gl - gmax), axis=-1, keepdims=True)
    g_idx = jnp.min(jnp.where(jnp.logical_and(is_grp, gl == gmax), lane, big), axis=-1, keepdims=True)
    g_w = 1.0 / gsum
    e_lo = N_EXP_GROUPS + g_idx * EXP_PER_GROUP
    in_grp = jnp.logical_and(lane >= e_lo, lane < e_lo + EXP_PER_GROUP)
    el = jnp.where(in_grp, logits, neg)
    emax = jnp.max(el, axis=-1, keepdims=True)
    ee = jnp.exp(el - emax)
    esum = jnp.sum(ee, axis=-1, keepdims=True)
    i0 = jnp.min(jnp.where(jnp.logical_and(in_grp, el == emax), lane, big), axis=-1, keepdims=True)
    el2 = jnp.where(lane == i0, neg, el)
    emax2 = jnp.max(el2, axis=-1, keepdims=True)
    i1 = jnp.min(jnp.where(jnp.logical_and(in_grp, el2 == emax2), lane, big), axis=-1, keepdims=True)
    w0 = 1.0 / esum
    w1 = jnp.exp(emax2 - emax) / esum
    wsum = w0 + w1
    gate0 = g_w * w0 / wsum
    gate1 = g_w * w1 / wsum
    id0 = i0 - N_EXP_GROUPS
    id1 = i1 - N_EXP_GROUPS
    route = jnp.where(lane == 0, id0, jnp.where(lane == 1, id1, jnp.where(lane == 2, gate0,
                      jnp.where(lane == 3, gate1, 0.0))))
    route_ref[...] = route


def _merge(ya, yb, ym, x, w_o, ln_g, ln_b, w_r_hi, w_r_lo, b_r, *, layer, tm):
    t = x.shape[0]
    row = lambda i: (i, 0)
    lay = lambda i: (layer, 0, 0)
    return pl.pallas_call(
        _merge_kernel,
        grid=(t // tm,),
        in_specs=[pl.BlockSpec((tm, D_MODEL), row)] * 4 + [
            pl.BlockSpec((None, D_MODEL, D_MODEL), lay),
            pl.BlockSpec((None, 1, D_MODEL), lay), pl.BlockSpec((None, 1, D_MODEL), lay),
            pl.BlockSpec((None, D_MODEL, LANES), lay), pl.BlockSpec((None, D_MODEL, LANES), lay),
            pl.BlockSpec((None, 1, LANES), lay)],
        out_specs=[pl.BlockSpec((tm * (D_MODEL // LANES), LANES), row), pl.BlockSpec((tm, LANES), row)],
        out_shape=[jax.ShapeDtypeStruct((t * (D_MODEL // LANES), LANES), F32),
                   jax.ShapeDtypeStruct((t, LANES), F32)],
        compiler_params=_params("parallel"),
        name="merge_ln_router",
    )(ya, yb, ym, x, w_o, ln_g, ln_b, w_r_hi, w_r_lo, b_r)


def _moe_kernel(bexp_ref, nused_ref, starts_ref, pstarts_ref, counts_ref, bord_ref, elist_ref, nexp_ref,
                tok_ref, row_ref, gate_ref, x_ref, win_hbm, wout_hbm, g_ref, b_ref, o_ref,
                xg_ref, xg2_ref, eo_ref, eo2_ref, y2_ref, win_buf, wout_buf, wsem, *, nblk, tc, layer):
    c = pl.program_id(0)
    i = pl.program_id(1)
    nslab = D_MODEL // LANES

    def tile(t):
        return pl.ds(pl.multiple_of(t * nslab, nslab), nslab)

    def weight_copies(expert, slot):
        return (pltpu.make_async_copy(win_hbm.at[layer, expert], win_buf.at[slot], wsem.at[slot, 0]),
                pltpu.make_async_copy(wout_hbm.at[layer, expert], wout_buf.at[slot], wsem.at[slot, 1]))

    def fetch(ordinal):
        @pl.when(ordinal < nexp_ref[c])
        def _():
            for cp in weight_copies(elist_ref[c * N_EXPERTS + ordinal], ordinal % MOE_WEIGHT_SLOTS):
                cp.start()

    @pl.when(i < nused_ref[c])
    def _():
        ordinal = bord_ref[c * nblk + i]
        slot = ordinal % MOE_WEIGHT_SLOTS
        first_block = jnp.logical_or(i == 0, bord_ref[jnp.maximum(c * nblk + i - 1, 0)] != ordinal)

        @pl.when(i == 0)
        def _():
            for ahead in range(MOE_WEIGHT_SLOTS - 1):
                fetch(ahead)

        @pl.when(first_block)
        def _():
            fetch(ordinal + MOE_WEIGHT_SLOTS - 1)
            for cp in weight_copies(0, slot):
                cp.wait()

        def block_pos(blk):
            tbl = c * N_EXPERTS + bexp_ref[c * nblk + blk]
            off = blk * MOE_BLOCK - pstarts_ref[tbl]
            return starts_ref[tbl] + off, counts_ref[tbl] - off

        def gather(first_pos, dst_ref, r_lo=0, r_hi=MOE_BLOCK):
            for r in range(r_lo, r_hi):
                dst_ref[r * nslab:(r + 1) * nslab, :] = x_ref[tile(tok_ref[0, 0, first_pos + r]), :]

        def scatter(first_pos, nreal, src_ref, r_lo=0, r_hi=MOE_BLOCK):
            for r in range(r_lo, r_hi):
                row = jnp.where(r < nreal, row_ref[0, 0, first_pos + r], EXP_TOPK * tc)
                y2_ref[tile(row), :] = gate_ref[0, 0, first_pos + r] * src_ref[r * nslab:(r + 1) * nslab, :]

        last_used = nused_ref[c] - 1
        pos0, nvalid = block_pos(i)
        next_pos0, _ = block_pos(jnp.minimum(i + 1, last_used))
        prev_pos0, prev_nvalid = block_pos(jnp.maximum(i - 1, 0))
        prev_nvalid = jnp.where(i > 0, prev_nvalid, 0)

        @pl.when(i == 0)
        def _():
            gather(pos0, xg_ref)

            @pl.when(c == 0)
            def _():
                eo2_ref[...] = jnp.zeros_like(eo2_ref)

        def run(cur_ref, nxt_ref, eo_cur, eo_prev):
            xb = _token_rows(cur_ref, 0, MOE_BLOCK).astype(BF16)
            pieces = []
            npiece = 4
            cols = 2 * EXP_FF // npiece
            rows_per = MOE_BLOCK // npiece
            for n in range(npiece):
                pieces.append(_dot(xb, win_buf[slot, :, n * cols:(n + 1) * cols]))
                gather(next_pos0, nxt_ref, n * rows_per, (n + 1) * rows_per)
                scatter(prev_pos0, prev_nvalid, eo_prev, n * rows_per, (n + 1) * rows_per)
            gu = jnp.concatenate(pieces, axis=-1)
            hid = _silu(gu[:, :EXP_FF]) * gu[:, EXP_FF:]
            eo = _dot(hid.astype(BF16), wout_buf[slot])
            for j in range(nslab):
                eo_cur[pl.ds(j, MOE_BLOCK, stride=nslab), :] = eo[:, j * LANES:(j + 1) * LANES]

            @pl.when(i == last_used)
            def _():
                scatter(pos0, nvalid, eo_cur)

        @pl.when(i % 2 == 0)
        def _():
            run(xg_ref, xg2_ref, eo_ref, eo2_ref)

        @pl.when(i % 2 == 1)
        def _():
            run(xg2_ref, xg_ref, eo2_ref, eo_ref)

    @pl.when(i == nblk - 1)
    def _():
        step = min(MOE_FINISH_ROWS, tc)

        def finish(blk, carry):
            t0 = pl.multiple_of(blk * step, step)
            ffn = _token_rows(y2_ref, t0, step) + _token_rows(y2_ref, tc + t0, step)
            o_ref[pl.ds(t0, step), :] = _layer_norm(ALPHA * _token_rows(x_ref, t0, step) + ffn,
                                                    g_ref[...], b_ref[...])
            return carry

        lax.fori_loop(0, tc // step, finish, 0)


def _token_rows(ref, t0, n):
    nslab = D_MODEL // LANES
    return jnp.concatenate([ref[pl.ds(t0 * nslab + j, n, stride=nslab), :] for j in range(nslab)], axis=-1)


def _moe(x1_tiles, tables, slot_arrays, w_in, w_out, ln_g, ln_b, *, layer, tc, nblk):
    nslab = D_MODEL // LANES
    t = x1_tiles.shape[0] // nslab
    nchunks = t // tc
    s = tc * EXP_TOPK
    kern = functools.partial(_moe_kernel, nblk=nblk, tc=tc, layer=layer)
    smem_row = lambda c, i, *_: (c, 0, 0)
    lay = lambda c, i, *_: (layer, 0, 0)
    grid_spec = pltpu.PrefetchScalarGridSpec(
        num_scalar_prefetch=len(tables),
        grid=(nchunks, nblk),
        in_specs=[pl.BlockSpec((1, 1, s + MOE_BLOCK), smem_row, memory_space=pltpu.SMEM)] * 3 + [
                  pl.BlockSpec((tc * nslab, LANES), lambda c, i, *_: (c, 0), pipeline_mode=pl.Buffered(1)),
                  pl.BlockSpec(memory_space=pl.ANY),
                  pl.BlockSpec(memory_space=pl.ANY),
                  pl.BlockSpec((None, 1, D_MODEL), lay),
                  pl.BlockSpec((None, 1, D_MODEL), lay)],
        out_specs=pl.BlockSpec((tc, D_MODEL), lambda c, i, *_: (c, 0)),
        scratch_shapes=[pltpu.VMEM((MOE_BLOCK * nslab, LANES), F32)] * 4 + [
                        pltpu.VMEM(((EXP_TOPK * tc + 1) * nslab, LANES), F32),
                        pltpu.VMEM((MOE_WEIGHT_SLOTS, D_MODEL, 2 * EXP_FF), BF16),
                        pltpu.VMEM((MOE_WEIGHT_SLOTS, EXP_FF, D_MODEL), BF16),
                        pltpu.SemaphoreType.DMA((MOE_WEIGHT_SLOTS, 2))],
    )
    return pl.pallas_call(
        kern,
        grid_spec=grid_spec,
        out_shape=jax.ShapeDtypeStruct((t, D_MODEL), F32),
        compiler_params=_params("arbitrary", "arbitrary"),
        name="moe_experts",
    )(*tables, *[a[:, None, :] for a in slot_arrays], x1_tiles, w_in, w_out, ln_g, ln_b)


def _dispatch(route, *, tc, nblk):
    t = route.shape[0]
    nchunks = t // tc
    s = tc * EXP_TOPK
    ids = route[:, 0:EXP_TOPK].astype(jnp.int32).reshape(nchunks, s)
    gates = route[:, EXP_TOPK:2 * EXP_TOPK].reshape(nchunks, s)
    slot = jnp.broadcast_to(jnp.arange(s, dtype=jnp.int32)[None, :], (nchunks, s))
    _, order, gate_sorted = lax.sort((ids, slot, gates), dimension=1, is_stable=True, num_keys=1)
    tok_sorted = order // EXP_TOPK
    row_sorted = (order % EXP_TOPK) * tc + tok_sorted
    counts =jnp.sum((ids[:, :, None] == jnp.arange(N_EXPERTS, dtype=jnp.int32)).astype(jnp.int32), axis=1)
    starts = jnp.cumsum(counts, axis=-1) - counts
    padded = (counts + MOE_BLOCK - 1) // MOE_BLOCK * MOE_BLOCK
    pends = jnp.cumsum(padded, axis=-1)
    pstarts = pends - padded
    blk_start = jnp.arange(nblk, dtype=jnp.int32) * MOE_BLOCK
    bexp = jnp.sum((pends[:, None, :] <= blk_start[None, :, None]).astype(jnp.int32), axis=-1)
    bexp = jnp.minimum(bexp, N_EXPERTS - 1).reshape(-1)
    nused = pends[:, -1] // MOE_BLOCK
    nonempty = counts > 0
    nexp = jnp.sum(nonempty.astype(jnp.int32), axis=-1)
    ordinal_of_expert = jnp.cumsum(nonempty.astype(jnp.int32), axis=-1) - 1
    elist = jnp.argsort(jnp.logical_not(nonempty), axis=-1, stable=True).astype(jnp.int32)
    bexp_2d = bexp.reshape(nchunks, nblk)
    bord = jnp.sum(jnp.where(bexp_2d[:, :, None] == jnp.arange(N_EXPERTS, dtype=jnp.int32),
                             ordinal_of_expert[:, None, :], 0), axis=-1).astype(jnp.int32)
    tables = (bexp, nused, starts.reshape(-1), pstarts.reshape(-1), counts.reshape(-1), bord.reshape(-1),
              elist.reshape(-1), nexp)
    pad = ((0, 0), (0, MOE_BLOCK))
    return tables, (jnp.pad(tok_sorted, pad), jnp.pad(row_sorted, pad), jnp.pad(gate_sorted, pad))


def _prep_weights(w_in, w_sconv_out, w_ssd_out, w_mem_k, w_mem_v, w_mem_out, w_o, w_exp_in, w_exp_out,
                  w_router_grp, b_router_grp, w_router_exp, b_router_exp, ssd_dt_bias, ssd_a_log, ssd_d):
    o_b, o_z, o_xbc, o_dt, o_q, o_g = 0, 3072, 5120, 8192, 8224, 9248
    w_main = jnp.concatenate([w_in[:, :, o_xbc:o_dt], w_in[:, :, o_b:o_z], w_in[:, :, o_z:o_xbc],
                              w_in[:, :, o_q:o_g], w_in[:, :, o_g:]], axis=-1).astype(BF16)
    pad_h = LANES - SSD_HEADS
    w_dt = jnp.pad(w_in[:, :, o_dt:o_q], ((0, 0), (0, 0), (0, pad_h))).astype(BF16)
    pad_r = LANES - N_EXP_GROUPS - N_EXPERTS
    w_r = jnp.pad(jnp.concatenate([w_router_grp, w_router_exp], axis=-1), ((0, 0), (0, 0), (0, pad_r)))
    w_r_hi = w_r.astype(BF16)
    w_r_lo = (w_r - w_r_hi.astype(F32)).astype(BF16)
    b_r = jnp.pad(jnp.concatenate([b_router_grp, b_router_exp], axis=-1), ((0, 0), (0, pad_r)))[:, None, :]
    head_of_col = jnp.arange(SSD_INNER, dtype=jnp.int32) // SSD_HEADDIM
    expand = (jnp.arange(LANES, dtype=jnp.int32)[:, None] == head_of_col[None, :]).astype(BF16)
    return dict(
        w_main=w_main, w_dt=w_dt, w_r_hi=w_r_hi, w_r_lo=w_r_lo, b_r=b_r, expand=expand,
        w_sconv_out=w_sconv_out.astype(BF16), w_ssd_out=w_ssd_out.astype(BF16),
        w_mem_kv=jnp.concatenate([w_mem_k, w_mem_v], axis=-1).astype(BF16),
        w_mem_out=w_mem_out.astype(BF16), w_o=w_o.astype(BF16),
        w_exp_in=w_exp_in.astype(BF16), w_exp_out=w_exp_out.astype(BF16),
        dt_bias=jnp.pad(ssd_dt_bias, ((0, 0), (0, pad_h)))[:, None, :],
        a_log=jnp.pad(ssd_a_log, ((0, 0), (0, pad_h)))[:, None, :],
        d_exp=jnp.repeat(ssd_d, SSD_HEADDIM, axis=-1)[:, None, :],
    )


def _moe_blocks(tc):
    return (tc * EXP_TOPK + N_EXPERTS * (MOE_BLOCK - 1) + MOE_BLOCK - 1) // MOE_BLOCK


def _decoder_layer(l, x, mem_k, mem_v, states_in, prev_layer, states_out, wp, p, cfg):
    nb, ls = cfg["nb"], cfg["ls"]
    sc_in, cv_in, h_in = states_in
    sc_out, cv_out, h_out = states_out
    proj = _matmul(x, wp["w_main"], l, cfg["tm_proj"], 2048)
    ya, sc_new = _branch_a(proj, sc_in, prev_layer, sc_out, p["w_sconv"], wp["w_sconv_out"],
                           layer=l, nb=nb, ls=ls, nseq=cfg["a_nseq"], lt=cfg["a_lt"], valid=cfg["a_valid"])
    yb, cv_new, h_new = _branch_b(proj, x, wp["w_dt"], cv_in, h_in, prev_layer, cv_out, h_out, p["w_ssd_conv"],
                                  p["b_ssd_conv"], wp["dt_bias"], wp["a_log"], wp["d_exp"], p["ssd_norm_w"],
                                  wp["expand"], wp["w_ssd_out"],
                                  layer=l, nb=nb, ls=ls, nseq=cfg["b_nseq"], lt=cfg["b_lt"], q=cfg["b_q"],
                                  valid=cfg["b_valid"])
    ym = _branch_m(proj, mem_k, mem_v, wp["w_mem_out"], layer=l, nb=nb, ls=ls, nseq=cfg["m_nseq"],
                   lt=cfg["m_lt"])
    x1, route = _merge(ya, yb, ym, x, wp["w_o"], p["ln1_g"], p["ln1_b"], wp["w_r_hi"], wp["w_r_lo"], wp["b_r"],
                       layer=l, tm=cfg["tm_merge"])
    tc = cfg["moe_tc"]
    nblk = _moe_blocks(tc)
    tables, slot_arrays = _dispatch(route, tc=tc, nblk=nblk)
    x2 = _moe(x1, tables, slot_arrays, wp["w_exp_in"], wp["w_exp_out"], p["ln2_g"], p["ln2_b"],
              layer=l, tc=tc, nblk=nblk)
    return x2, (sc_new, cv_new, h_new)


PROMPT_CFG = dict(tm_proj=1024, a_nseq=1, a_lt=512, b_nseq=1, b_lt=256, b_q=SSD_CHUNK, m_nseq=1, m_lt=512,
                  tm_merge=512, moe_tc=2048)
SAMPLE_PAD = SUBLANES


def _zero_states(nb):
    return (jnp.zeros((DEPTH, nb, SC_CONV_W - 1, SC_DIM), F32),
            jnp.zeros((DEPTH, nb, SSD_CONV_W - 1, SSD_CONV_DIM), F32),
            jnp.zeros((DEPTH, nb, SSD_HEADS, SSD_HEADDIM, SSD_STATE), F32))


def kernel(x_prompt, x_sample, mem_prompt, cache_mem_k, cache_mem_v, state_sconv, state_ssd_conv, state_ssd, w_in, w_sconv, w_sconv_out, w_ssd_conv, b_ssd_conv, ssd_dt_bias, ssd_a_log, ssd_d, ssd_norm_w, w_ssd_out, w_mem_k, w_mem_v, w_mem_out, w_o, ln1_g, ln1_b, w_router_grp, b_router_grp, w_router_exp, b_router_exp, w_exp_in, w_exp_out, ln2_g, ln2_b):
    bp, seq, _ = x_prompt.shape
    bs, dec_seq, _ = x_sample.shape
    wp = _prep_weights(w_in, w_sconv_out, w_ssd_out, w_mem_k, w_mem_v, w_mem_out, w_o, w_exp_in, w_exp_out,
                       w_router_grp, b_router_grp, w_router_exp, b_router_exp, ssd_dt_bias, ssd_a_log, ssd_d)
    p = dict(w_sconv=w_sconv, w_ssd_conv=w_ssd_conv, b_ssd_conv=b_ssd_conv[:, None, :],
             ssd_norm_w=ssd_norm_w[:, None, :], ln1_g=ln1_g[:, None, :], ln1_b=ln1_b[:, None, :],
             ln2_g=ln2_g[:, None, :], ln2_b=ln2_b[:, None, :])

    cfg_p = dict(PROMPT_CFG, nb=bp, ls=seq, a_valid=PROMPT_CFG["a_lt"], b_valid=PROMPT_CFG["b_lt"])
    mem_rows = mem_prompt.reshape(bp * MEM_LEN, D_MODEL)
    zeros_in = (jnp.zeros((1, bp, SC_CONV_W - 1, SC_DIM), F32),
                jnp.zeros((1, bp, SSD_CONV_W - 1, SSD_CONV_DIM), F32),
                jnp.zeros((1, bp, SSD_HEADS, SSD_HEADDIM, SSD_STATE), F32))
    yp = x_prompt.reshape(bp * seq, D_MODEL)
    mk_p = jnp.zeros((DEPTH, bp, MEM_LEN, MEM_HEADS, MEM_HEAD_DIM), F32)
    mv_p = jnp.zeros((DEPTH, bp, MEM_LEN, MEM_HEADS, MEM_HEAD_DIM), F32)
    st_p = _zero_states(bp)
    for l in range(DEPTH):
        mk_p = _memkv(mem_rows, wp["w_mem_kv"], l, 0, mk_p, nb=bp, nbt=4)
        mv_p = _memkv(mem_rows, wp["w_mem_kv"], l, 1, mv_p, nb=bp, nbt=4)
        yp, st_p = _decoder_layer(l, yp, mk_p, mv_p, zeros_in, 0, st_p, wp, p, cfg_p)

    ls = SAMPLE_PAD
    cfg_s = dict(tm_proj=bs * ls, nb=bs, ls=ls, a_nseq=16, a_lt=ls, a_valid=dec_seq, b_nseq=4, b_lt=ls, b_q=ls,
                 b_valid=dec_seq, m_nseq=2, m_lt=ls, tm_merge=512, moe_tc=bs * ls)
    ys = jnp.pad(x_sample, ((0, 0), (0, ls - dec_seq), (0, 0))).reshape(bs * ls, D_MODEL)
    st_s = _zero_states(bs)
    for l in range(DEPTH):
        ys, st_s = _decoder_layer(l, ys, cache_mem_k, cache_mem_v, (state_sconv, state_ssd_conv, state_ssd), l,
                                  st_s, wp, p, cfg_s)

    y_prompt = yp.reshape(bp, seq, D_MODEL)
    y_sample = ys.reshape(bs, ls, D_MODEL)[:, :dec_seq]
    return (y_prompt, y_sample, mk_p, mv_p, st_p[0], st_p[1], st_p[2], st_s[0], st_s[1], st_s[2])
```
